```python
import jax, jax.numpy as jnp
from jax import lax
import numpy as np

D_MODEL = 2048
BATCH = 2
SEQ = 16384
DEPTH = 2

N_MIXERS = 2
FOX_HEADS = 16
FOX_HEAD_DIM = D_MODEL // FOX_HEADS
FOX_BLOCK_Q = 128
RET_HEADS = 8
RET_QK_DIM = D_MODEL // RET_HEADS
RET_V_DIM = 2 * D_MODEL // RET_HEADS
RET_CHUNK = 128
RET_ROT_BASE = 10000.0
D_FF = 5632
CONV_WIDTH = 3
NORM_EPS = 1e-6

kernel_name = "fox_retnet_interleaved_convffn_sandwich"


def rmsnorm(x, g):
    xf = x.astype(jnp.float32)
    y = xf * lax.rsqrt(jnp.mean(xf * xf, axis=-1, keepdims=True) + NORM_EPS)
    return (y * g.astype(jnp.float32)).astype(x.dtype)


def fox_attention(h, w_in, b_f, w_o):
    B, S, _ = h.shape
    H, dh, D = FOX_HEADS, FOX_HEAD_DIM, D_MODEL
    proj = h @ w_in
    q = proj[..., :D].reshape(B, S, H, dh).transpose(0, 2, 1, 3)
    k = proj[..., D:2 * D].reshape(B, S, H, dh).transpose(0, 2, 1, 3)
    v = proj[..., 2 * D:3 * D].reshape(B, S, H, dh).transpose(0, 2, 1, 3)
    log_f = jax.nn.log_sigmoid((proj[..., 3 * D:] + b_f).astype(jnp.float32))
    cum = jnp.cumsum(log_f, axis=1).transpose(0, 2, 1)
    scale = dh ** -0.5
    kpos = jnp.arange(S)

    def block(i):
        start = i * FOX_BLOCK_Q
        qb = lax.dynamic_slice_in_dim(q, start, FOX_BLOCK_Q, axis=2)
        cb = lax.dynamic_slice_in_dim(cum, start, FOX_BLOCK_Q, axis=2)
        qpos = start + jnp.arange(FOX_BLOCK_Q)
        logits = (jnp.einsum('bhqd,bhkd->bhqk', qb, k).astype(jnp.float32) * scale
                  + cb[..., :, None] - cum[..., None, :])
        logits = jnp.where(kpos[None, :] <= qpos[:, None], logits, -jnp.inf)
        p = jax.nn.softmax(logits, axis=-1).astype(v.dtype)
        return jnp.einsum('bhqk,bhkd->bhqd', p, v)

    out = lax.map(block, jnp.arange(S // FOX_BLOCK_Q))
    out = out.transpose(1, 0, 3, 2, 4).reshape(B, S, D)
    return out @ w_o


def rotate_every_two(x, sin, cos):
    x1 = x[..., 0::2]
    x2 = x[..., 1::2]
    s = sin[None, :, None, :]
    c = cos[None, :, None, :]
    return jnp.stack([x1 * c - x2 * s, x1 * s + x2 * c], axis=-1).reshape(x.shape)


def retention(h, w_in, w_o):
    B, S, _ = h.shape
    H, dk, dv, D, C = RET_HEADS, RET_QK_DIM, RET_V_DIM, D_MODEL, RET_CHUNK
    NC = S // C
    proj = h @ w_in
    q = proj[..., :D].reshape(B, S, H, dk)
    k = proj[..., D:2 * D].reshape(B, S, H, dk)
    v = proj[..., 2 * D:4 * D].reshape(B, S, H, dv)
    g = proj[..., 4 * D:]

    theta = 1.0 / (RET_ROT_BASE ** jnp.linspace(0.0, 1.0, dk // 2, dtype=jnp.float32))
    ang = jnp.arange(S, dtype=jnp.float32)[:, None] * theta[None, :]
    sin, cos = jnp.sin(ang), jnp.cos(ang)
    q = rotate_every_two(q.astype(jnp.float32), sin, cos)
    k = rotate_every_two(k.astype(jnp.float32), sin, cos) * (dk ** -0.5)
    v = v.astype(jnp.float32)

    def to_chunks(t):
        return t.reshape(B, NC, C, H, t.shape[-1]).transpose(1, 0, 3, 2, 4)

    qc, kc, vc = to_chunks(q), to_chunks(k), to_chunks(v)

    log_gamma = jnp.log1p(-(2.0 ** (-5.0 - jnp.arange(H, dtype=jnp.float32))))
    idx = jnp.arange(C, dtype=jnp.float32)
    rel = idx[:, None] - idx[None, :]
    decay_intra = jnp.where(rel[None] >= 0,
                            jnp.exp(jnp.maximum(rel, 0.0)[None] * log_gamma[:, None, None]),
                            0.0)
    decay_q = jnp.exp((idx + 1.0)[None, :] * log_gamma[:, None])
    decay_k = jnp.exp((C - 1.0 - idx)[None, :] * log_gamma[:, None])
    decay_chunk = jnp.exp(C * log_gamma)

    def step(state, xs):
        qb, kb, vb = xs
        scores = jnp.einsum('bhnd,bhmd->bhnm', qb, kb) * decay_intra[None]
        o = (jnp.einsum('bhnm,bhmv->bhnv', scores, vb)
             + jnp.einsum('bhnd,bhdv->bhnv', qb, state) * decay_q[None, :, :, None])
        new_state = (decay_chunk[None, :, None, None] * state
                     + jnp.einsum('bhmd,bhmv->bhdv', kb * decay_k[None, :, :, None], vb))
        return new_state, o

    state0 = jnp.zeros((B, H, dk, dv), jnp.float32)
    _, o = lax.scan(step, state0, (qc, kc, vc))
    o = o.transpose(1, 0, 3, 2, 4).reshape(B, S, H, dv)
    o = o * lax.rsqrt(jnp.mean(o * o, axis=-1, keepdims=True) + NORM_EPS)
    o = o.reshape(B, S, H * dv).astype(h.dtype) * jax.nn.silu(g)
    return o @ w_o


def conv_gated_mlp(h, w_up, conv_w, conv_b, w_down):
    u = h @ w_up
    u = lax.conv_general_dilated(
        u, conv_w[:, None, :], window_strides=(1,),
        padding=((CONV_WIDTH - 1, 0),),
        dimension_numbers=('NWC', 'WIO', 'NWC'),
        feature_group_count=u.shape[-1]) + conv_b
    a, b = u[..., :D_FF], u[..., D_FF:]
    return (jax.nn.silu(a) * b) @ w_down


def setup_inputs(seed: int = 0) -> dict:
    key = jax.random.key(seed)
    ks = jax.random.split(key, 16)
    D, F = D_MODEL, D_FF
    n_fox = (DEPTH + 1) // 2
    n_ret = DEPTH // 2
    nrm = jax.random.normal
    x = nrm(ks[0], (BATCH, SEQ, D), jnp.float32)
    norm_g = 1.0 + 0.05 * nrm(ks[1], (DEPTH, 4, D), jnp.float32)
    fox_qkv = nrm(ks[2], (n_fox, D, 3 * D), jnp.float32) * D ** -0.5
    fox_f = nrm(ks[3], (n_fox, D, FOX_HEADS), jnp.float32) * (0.1 * D ** -0.5)
    fox_w_in = jnp.concatenate([fox_qkv, fox_f], axis=-1)
    fox_b_f = (jnp.linspace(1.0, 6.0, FOX_HEADS, dtype=jnp.float32)[None, :]
               + 0.1 * nrm(ks[4], (n_fox, FOX_HEADS), jnp.float32))
    fox_w_o = nrm(ks[5], (n_fox, D, D), jnp.float32) * D ** -0.5
    ret_w_in = nrm(ks[6], (n_ret, D, 6 * D), jnp.float32) * D ** -0.5
    ret_w_o = nrm(ks[7], (n_ret, 2 * D, D), jnp.float32) * (2 * D) ** -0.5
    ffn_w_up = nrm(ks[8], (DEPTH, D, 2 * F), jnp.float32) * D ** -0.5
    ffn_conv_w = nrm(ks[9], (DEPTH, CONV_WIDTH, 2 * F), jnp.float32) * CONV_WIDTH ** -0.5
    ffn_conv_b = 0.02 * nrm(ks[10], (DEPTH, 2 * F), jnp.float32)
    ffn_w_down = nrm(ks[11], (DEPTH, F, D), jnp.float32) * F ** -0.5
    return {"x": x, "norm_g": norm_g, "fox_w_in": fox_w_in, "fox_b_f": fox_b_f,
            "fox_w_o": fox_w_o, "ret_w_in": ret_w_in, "ret_w_o": ret_w_o,
            "ffn_w_up": ffn_w_up, "ffn_conv_w": ffn_conv_w, "ffn_conv_b": ffn_conv_b,
            "ffn_w_down": ffn_w_down}


def reference(x, norm_g, fox_w_in, fox_b_f, fox_w_o, ret_w_in, ret_w_o,
              ffn_w_up, ffn_conv_w, ffn_conv_b, ffn_w_down):
    for i in range(DEPTH):
        g = norm_g[i]
        h = rmsnorm(x, g[0])
        j = i // N_MIXERS
        if i % N_MIXERS == 0:
            m = fox_attention(h, fox_w_in[j], fox_b_f[j], fox_w_o[j])
        else:
            m = retention(h, ret_w_in[j], ret_w_o[j])
        x = x + rmsnorm(m, g[1])
        h = rmsnorm(x, g[2])
        f = conv_gated_mlp(h, ffn_w_up[i], ffn_conv_w[i], ffn_conv_b[i], ffn_w_down[i])
        x = x + rmsnorm(f, g[3])
    return x
```

```python
import functools
import math

import jax
import jax.numpy as jnp
from jax import lax
from jax.experimental import pallas as pl
from jax.experimental.pallas import tpu as pltpu

NORM_EPS = 1e-6
RET_HEADS = 8
RET_ROT_BASE = 10000.0
LOG2E = 1.4426950408889634
NEG_BIG = -1e30

LANES = 128
BF16_SUBLANES = 16
VMEM_LIMIT = 56 * 1024 * 1024

F32 = jnp.float32
BF16 = jnp.bfloat16


def _params(sem, vmem=VMEM_LIMIT):
    return pltpu.CompilerParams(dimension_semantics=sem, vmem_limit_bytes=vmem)


def _rms(x, g):
    ms = jnp.mean(x * x, axis=-1, keepdims=True)
    return x * lax.rsqrt(ms + NORM_EPS) * g


def _norm_kernel(x_ref, g_ref, o_ref):
    o_ref[...] = _rms(x_ref[...], g_ref[...]).astype(o_ref.dtype)


def norm_cast(x, g, tm):
    m, d = x.shape
    return pl.pallas_call(
        _norm_kernel,
        out_shape=jax.ShapeDtypeStruct((m, d), BF16),
        grid=(m // tm,),
        in_specs=[pl.BlockSpec((tm, d), lambda i: (i, 0)),
                  pl.BlockSpec((1, d), lambda i: (0, 0))],
        out_specs=pl.BlockSpec((tm, d), lambda i: (i, 0)),
        compiler_params=_params(("parallel",)),
        name="norm_cast",
    )(x, g)


def _fox_proj_kernel(h_ref, w_ref, o_ref, *, n_q_tiles, q_scale):
    j = pl.program_id(1)
    acc = jnp.dot(h_ref[...], w_ref[...], preferred_element_type=F32)
    scale = jnp.where(j < n_q_tiles, q_scale, 1.0).astype(F32)
    o_ref[...] = (acc * scale).astype(o_ref.dtype)


def fox_proj(h, w, d_model, q_scale, tm, tn):
    m, k = h.shape
    n = w.shape[1]
    kern = functools.partial(_fox_proj_kernel, n_q_tiles=d_model // tn, q_scale=q_scale)
    return pl.pallas_call(
        kern,
        out_shape=jax.ShapeDtypeStruct((m, n), BF16),
        grid=(m // tm, n // tn),
        in_specs=[pl.BlockSpec((tm, k), lambda i, j: (i, 0)),
                  pl.BlockSpec((k, tn), lambda i, j: (0, j))],
        out_specs=pl.BlockSpec((tm, tn), lambda i, j: (i, j)),
        compiler_params=_params(("parallel", "arbitrary")),
        name="fox_proj",
    )(h, w)


def _split3(x):
    p1 = x.astype(BF16)
    r1 = x - p1.astype(F32)
    p2 = r1.astype(BF16)
    p3 = (r1 - p2.astype(F32)).astype(BF16)
    return p1, p2, p3


def _forget_kernel(h_ref, wf_ref, bf_ref, sel_ref, const_ref, qx_ref, kx_ref, carry_ref,
                   *, n_heads):
    t = pl.program_id(1)
    tm = h_ref.shape[0]

    @pl.when(t == 0)
    def _():
        carry_ref[...] = jnp.zeros_like(carry_ref)

    z = jnp.dot(h_ref[...], wf_ref[...], preferred_element_type=F32) + bf_ref[...]
    log_f = jnp.minimum(z, 0.0) - jnp.log1p(jnp.exp(-jnp.abs(z)))
    lane = lax.broadcasted_iota(jnp.int32, z.shape, 1)
    log_f = jnp.where(lane < n_heads, log_f * LOG2E, 0.0)

    row = lax.broadcasted_iota(jnp.int32, (tm, tm), 0)
    col = lax.broadcasted_iota(jnp.int32, (tm, tm), 1)
    tri = jnp.where(row >= col, 1.0, 0.0).astype(BF16)
    cum = carry_ref[...]
    for piece in _split3(log_f):
        cum = cum + jnp.dot(tri, piece, preferred_element_type=F32)
    carry_ref[...] = cum[tm - 1:tm, :]

    qx = const_ref[0:1, :].astype(F32)
    kx = const_ref[1:2, :].astype(F32)
    for idx, piece in enumerate(_split3(cum)):
        qx = qx + jnp.dot(piece, sel_ref[idx], preferred_element_type=F32)
        kx = kx + jnp.dot(piece, sel_ref[3 + idx], preferred_element_type=F32)
    qx_ref[...] = qx.astype(qx_ref.dtype)
    kx_ref[...] = kx.astype(kx_ref.dtype)


def forget_features(h, wf, bf, n_heads, batch, seq, tm):
    m, d = h.shape
    width = n_heads * LANES
    head = jnp.arange(LANES)[:, None]
    colid = jnp.arange(width)[None, :]
    sel = []
    for sign, base in ((1.0, 0), (-1.0, 3)):
        for idx in range(3):
            sel.append(jnp.where((colid == head * LANES + base + idx) & (head < n_heads), sign, 0.0))
    sel = jnp.stack(sel).astype(BF16)
    within = colid % LANES
    const = jnp.concatenate([
        jnp.where((within >= 3) & (within < 6), 1.0, 0.0),
        jnp.where(within < 3, 1.0, 0.0),
    ], axis=0).astype(BF16)
    tiles = seq // tm
    kern = functools.partial(_forget_kernel, n_heads=n_heads)
    return pl.pallas_call(
        kern,
        out_shape=(jax.ShapeDtypeStruct((m, width), BF16),
                   jax.ShapeDtypeStruct((m, width), BF16)),
        grid=(batch, tiles),
        in_specs=[pl.BlockSpec((tm, d), lambda b, t: (b * tiles + t, 0)),
                  pl.BlockSpec((d, LANES), lambda b, t: (0, 0)),
                  pl.BlockSpec((1, LANES), lambda b, t: (0, 0)),
                  pl.BlockSpec((6, LANES, width), lambda b, t: (0, 0, 0)),
                  pl.BlockSpec((2, width), lambda b, t: (0, 0))],
        out_specs=(pl.BlockSpec((tm, width), lambda b, t: (b * tiles + t, 0)),
                   pl.BlockSpec((tm, width), lambda b, t: (b * tiles + t, 0))),
        scratch_shapes=[pltpu.VMEM((1, LANES), F32)],
        compiler_params=_params(("arbitrary", "arbitrary")),
        name="forget_features",
    )(h, wf, bf, sel, const)


def _flash_kernel(q_ref, qx_ref, k_ref, kx_ref, v_ref, o_ref, m_ref, l_ref, acc_ref, *, tq):
    i = pl.program_id(2)
    q2 = jnp.concatenate([q_ref[...], qx_ref[...]], axis=1)
    m_ref[...] = jnp.full_like(m_ref, NEG_BIG)
    l_ref[...] = jnp.zeros_like(l_ref)
    acc_ref[...] = jnp.zeros_like(acc_ref)

    def block(j, masked):
        off = pl.multiple_of(j * tq, tq)
        k2 = jnp.concatenate([k_ref[pl.ds(off, tq), :], kx_ref[pl.ds(off, tq), :]], axis=1)
        s = lax.dot_general(q2, k2, (((1,), (1,)), ((), ())), preferred_element_type=F32)
        if masked:
            row = lax.broadcasted_iota(jnp.int32, s.shape, 0)
            col = lax.broadcasted_iota(jnp.int32, s.shape, 1)
            s = jnp.where(row >= col, s, NEG_BIG)
        m_old = m_ref[...]
        m_new = jnp.maximum(m_old, jnp.max(s, axis=1, keepdims=True))
        p = jnp.exp2(s - m_new)
        alpha = jnp.exp2(m_old - m_new)
        l_ref[...] = alpha * l_ref[...] + jnp.sum(p, axis=1, keepdims=True)
        acc_ref[...] = alpha * acc_ref[...] + jnp.dot(
            p.astype(BF16), v_ref[pl.ds(off, tq), :], preferred_element_type=F32)
        m_ref[...] = m_new

    def full_block(j, carry):
        block(j, masked=False)
        return carry

    lax.fori_loop(0, i, full_block, 0)
    block(i, masked=True)
    o_ref[...] = (acc_ref[...] / l_ref[...]).astype(o_ref.dtype)


def flash_attention(qkv, qx, kx, n_heads, batch, seq, tq):
    m = qkv.shape[0]
    dh = LANES
    nq = seq // tq
    kern = functools.partial(_flash_kernel, tq=tq)
    return pl.pallas_call(
        kern,
        out_shape=jax.ShapeDtypeStruct((m, n_heads * dh), BF16),
        grid=(batch, n_heads, nq),
        in_specs=[pl.BlockSpec((tq, dh), lambda b, h, i: (b * nq + i, h)),
                  pl.BlockSpec((tq, dh), lambda b, h, i: (b * nq + i, h)),
                  pl.BlockSpec((seq, dh), lambda b, h, i: (b, n_heads + h)),
                  pl.BlockSpec((seq, dh), lambda b, h, i: (b, h)),
                  pl.BlockSpec((seq, dh), lambda b, h, i: (b, 2 * n_heads + h))],
        out_specs=pl.BlockSpec((tq, dh), lambda b, h, i: (b * nq + i, h)),
        scratch_shapes=[pltpu.VMEM((tq, 1), F32), pltpu.VMEM((tq, 1), F32),
                        pltpu.VMEM((tq, dh), F32)],
        compiler_params=_params(("arbitrary", "arbitrary", "arbitrary")),
        name="fox_flash",
    )(qkv, qx, qkv, kx, qkv)


def _post_kernel(a_ref, w_ref, x_ref, gp_ref, gn_ref, xo_ref, ho_ref):
    mix = jnp.dot(a_ref[...], w_ref[...], preferred_element_type=F32)
    x_new = x_ref[...] + _rms(mix, gp_ref[...])
    xo_ref[...] = x_new
    ho_ref[...] = _rms(x_new, gn_ref[...]).astype(ho_ref.dtype)


def post_mixer(a, w, x, g_post, g_next, tm):
    m, k = a.shape
    d = w.shape[1]
    return pl.pallas_call(
        _post_kernel,
        out_shape=(jax.ShapeDtypeStruct((m, d), F32), jax.ShapeDtypeStruct((m, d), BF16)),
        grid=(m // tm,),
        in_specs=[pl.BlockSpec((tm, k), lambda i: (i, 0)),
                  pl.BlockSpec((k, d), lambda i: (0, 0), pipeline_mode=pl.Buffered(1)),
                  pl.BlockSpec((tm, d), lambda i: (i, 0)),
                  pl.BlockSpec((1, d), lambda i: (0, 0)),
                  pl.BlockSpec((1, d), lambda i: (0, 0))],
        out_specs=(pl.BlockSpec((tm, d), lambda i: (i, 0)),
                   pl.BlockSpec((tm, d), lambda i: (i, 0))),
        compiler_params=_params(("parallel",)),
        name="post_mixer",
    )(a, w, x, g_post, g_next)


def _ffn_kernel(h_ref, halo_ref, x_ref, wa_ref, wb_ref, wd_ref, cwa_ref, cwb_ref,
                cba_ref, cbb_ref, gp_ref, gn_ref, xo_ref, *rest, tiles_per_seq, emit_next):
    if emit_next:
        ho_ref, hs_ref, acc_ref = rest
    else:
        hs_ref, acc_ref = rest
    i = pl.program_id(0)
    j = pl.program_id(1)
    nj = pl.num_programs(1)
    tm = h_ref.shape[0]
    pad = halo_ref.shape[0]

    @pl.when(j == 0)
    def _():
        halo = halo_ref[...]
        hs_ref[0:pad, :] = jnp.where(i % tiles_per_seq != 0, halo, jnp.zeros_like(halo))
        hs_ref[pad:pad + tm, :] = h_ref[...]
        acc_ref[...] = jnp.zeros_like(acc_ref)

    hs = hs_ref[...]

    def conv(w_ref, cw_ref, cb_ref):
        u = jnp.dot(hs, w_ref[...], preferred_element_type=F32)
        u1 = pltpu.roll(u, 1, 0)
        u2 = pltpu.roll(u, 2, 0)
        cw = cw_ref[...]
        return (u[pad:, :] * cw[2:3, :] + u1[pad:, :] * cw[1:2, :]
                + u2[pad:, :] * cw[0:1, :] + cb_ref[...])

    a = conv(wa_ref, cwa_ref, cba_ref)
    b = conv(wb_ref, cwb_ref, cbb_ref)
    act = (a * jax.nn.sigmoid(a) * b).astype(BF16)
    acc_ref[...] += jnp.dot(act, wd_ref[...], preferred_element_type=F32)

    @pl.when(j == nj - 1)
    def _():
        x_new = x_ref[...] + _rms(acc_ref[...], gp_ref[...])
        xo_ref[...] = x_new
        if emit_next:
            ho_ref[...] = _rms(x_new, gn_ref[...]).astype(ho_ref.dtype)


def conv_ffn(h, x, w_up, conv_w, conv_b, w_down, g_post, g_next, seq, tm, tf, emit_next):
    m, d = h.shape
    f = w_down.shape[0]
    nf = f // tf
    pad = BF16_SUBLANES
    tiles_per_seq = seq // tm
    halo_blocks = tm // pad
    kern = functools.partial(_ffn_kernel, tiles_per_seq=tiles_per_seq, emit_next=emit_next)
    out_shape = [jax.ShapeDtypeStruct((m, d), F32)]
    out_specs = [pl.BlockSpec((tm, d), lambda i, j: (i, 0))]
    if emit_next:
        out_shape.append(jax.ShapeDtypeStruct((m, d), BF16))
        out_specs.append(pl.BlockSpec((tm, d), lambda i, j: (i, 0)))
    res = pl.pallas_call(
        kern,
        out_shape=tuple(out_shape),
        grid=(m // tm, nf),
        in_specs=[pl.BlockSpec((tm, d), lambda i, j: (i, 0)),
                  pl.BlockSpec((pad, d), lambda i, j: (jnp.maximum(i * halo_blocks - 1, 0), 0)),
                  pl.BlockSpec((tm, d), lambda i, j: (i, 0)),
                  pl.BlockSpec((d, tf), lambda i, j: (0, j)),
                  pl.BlockSpec((d, tf), lambda i, j: (0, nf + j)),
                  pl.BlockSpec((tf, d), lambda i, j: (j, 0)),
                  pl.BlockSpec((conv_w.shape[0], tf), lambda i, j: (0, j)),
                  pl.BlockSpec((conv_w.shape[0], tf), lambda i, j: (0, nf + j)),
                  pl.BlockSpec((1, tf), lambda i, j: (0, j)),
                  pl.BlockSpec((1, tf), lambda i, j: (0, nf + j)),
                  pl.BlockSpec((1, d), lambda i, j: (0, 0)),
                  pl.BlockSpec((1, d), lambda i, j: (0, 0))],
        out_specs=tuple(out_specs),
        scratch_shapes=[pltpu.VMEM((pad + tm, d), BF16), pltpu.VMEM((tm, d), F32)],
        compiler_params=_params(("arbitrary", "arbitrary")),
        name="conv_ffn",
    )(h, h, x, w_up, w_up, w_down, conv_w, conv_w, conv_b, conv_b, g_post, g_next)
    return res if emit_next else (res[0], None)


def _ret_proj_kernel(h_ref, w_ref, cos_ref, sin_ref, o_ref, *, n_q_tiles, n_qk_tiles,
                     n_qkv_tiles, k_scale):
    j = pl.program_id(1)
    acc = jnp.dot(h_ref[...], w_ref[...], preferred_element_type=F32)
    tn = acc.shape[1]

    @pl.when(j < n_qk_tiles)
    def _():
        scale = jnp.where(j < n_q_tiles, 1.0, k_scale).astype(F32)
        c = cos_ref[...] * scale
        s = sin_ref[...] * scale
        for hh in range(tn // (2 * LANES)):
            lo = hh * 2 * LANES
            x1 = acc[:, lo:lo + LANES]
            x2 = acc[:, lo + LANES:lo + 2 * LANES]
            o_ref[:, lo:lo + LANES] = (x1 * c - x2 * s).astype(o_ref.dtype)
            o_ref[:, lo + LANES:lo + 2 * LANES] = (x1 * s + x2 * c).astype(o_ref.dtype)

    @pl.when((j >= n_qk_tiles) & (j < n_qkv_tiles))
    def _():
        o_ref[...] = acc.astype(o_ref.dtype)

    @pl.when(j >= n_qkv_tiles)
    def _():
        o_ref[...] = (acc * jax.nn.sigmoid(acc)).astype(o_ref.dtype)


def ret_proj(h, w, cos, sin, d_model, k_scale, seq, tm, tn):
    m, k = h.shape
    n = w.shape[1]
    tiles_per_seq = seq // tm
    half = cos.shape[1]
    kern = functools.partial(_ret_proj_kernel, n_q_tiles=d_model // tn,
                             n_qk_tiles=2 * d_model // tn, n_qkv_tiles=4 * d_model // tn,
                             k_scale=k_scale)
    return pl.pallas_call(
        kern,
        out_shape=jax.ShapeDtypeStruct((m, n), BF16),
        grid=(m // tm, n // tn),
        in_specs=[pl.BlockSpec((tm, k), lambda i, j: (i, 0)),
                  pl.BlockSpec((k, tn), lambda i, j: (0, j)),
                  pl.BlockSpec((tm, half), lambda i, j: (i % tiles_per_seq, 0)),
                  pl.BlockSpec((tm, half), lambda i, j: (i % tiles_per_seq, 0))],
        out_specs=pl.BlockSpec((tm, tn), lambda i, j: (i, j)),
        compiler_params=_params(("parallel", "arbitrary")),
        name="ret_proj",
    )(h, w, cos, sin)


def _retention_kernel(lg_ref, q_ref, k_ref, v_ref, g_ref, o_ref, state_ref, mask_ref):
    h = pl.program_id(1)
    c = pl.program_id(2)
    chunk = q_ref.shape[0]
    lg = lg_ref[h]

    @pl.when(c == 0)
    def _():
        state_ref[...] = jnp.zeros_like(state_ref)
        n = lax.broadcasted_iota(jnp.int32, (chunk, chunk), 0)
        mm = lax.broadcasted_iota(jnp.int32, (chunk, chunk), 1)
        rel = (n - mm).astype(F32)
        mask_ref[...] = jnp.where(rel >= 0, jnp.exp(jnp.maximum(rel, 0.0) * lg), 0.0)

    qb = q_ref[...]
    kb = k_ref[...]
    vb = v_ref[...]
    idx = lax.broadcasted_iota(jnp.int32, (chunk, 1), 0).astype(F32)
    decay_q = jnp.exp((idx + 1.0) * lg)
    decay_k = jnp.exp((chunk - 1.0 - idx) * lg)
    decay_chunk = jnp.exp(jnp.full((1, 1), chunk, F32) * lg)

    scores = lax.dot_general(qb, kb, (((1,), (1,)), ((), ())), preferred_element_type=F32)
    scores = scores * mask_ref[...]
    state = state_ref[...]
    o = (jnp.dot(scores.astype(BF16), vb, preferred_element_type=F32)
         + jnp.dot(qb, state.astype(BF16), preferred_element_type=F32) * decay_q)
    kd = (kb.astype(F32) * decay_k).astype(BF16)
    state_ref[...] = decay_chunk * state + lax.dot_general(
        kd, vb, (((0,), (0,)), ((), ())), preferred_element_type=F32)

    o = o * lax.rsqrt(jnp.mean(o * o, axis=-1, keepdims=True) + NORM_EPS)
    o_ref[...] = (o * g_ref[...].astype(F32)).astype(o_ref.dtype)


def retention_mix(proj, log_gamma, n_heads, d_model, batch, seq, chunk):
    m = proj.shape[0]
    dk = d_model // n_heads
    dv = 2 * d_model // n_heads
    nc = seq // chunk
    grid_spec = pltpu.PrefetchScalarGridSpec(
        num_scalar_prefetch=1,
        grid=(batch, n_heads, nc),
        in_specs=[pl.BlockSpec((chunk, dk), lambda b, h, c, lg: (b * nc + c, h)),
                  pl.BlockSpec((chunk, dk), lambda b, h, c, lg: (b * nc + c, n_heads + h)),
                  pl.BlockSpec((chunk, dv), lambda b, h, c, lg: (b * nc + c, n_heads + h)),
                  pl.BlockSpec((chunk, dv), lambda b, h, c, lg: (b * nc + c, 2 * n_heads + h))],
        out_specs=pl.BlockSpec((chunk, dv), lambda b, h, c, lg: (b * nc + c, h)),
        scratch_shapes=[pltpu.VMEM((dk, dv), F32), pltpu.VMEM((chunk, chunk), F32)],
    )
    return pl.pallas_call(
        _retention_kernel,
        out_shape=jax.ShapeDtypeStruct((m, n_heads * dv), BF16),
        grid_spec=grid_spec,
        compiler_params=_params(("arbitrary", "arbitrary", "arbitrary")),
        name="retention_mix",
    )(log_gamma, proj, proj, proj, proj)


def _pick(total, pref):
    if total <= pref:
        return total
    t = pref
    while total % t:
        t -= LANES
    return t


def kernel(x, norm_g, fox_w_in, fox_b_f, fox_w_o, ret_w_in, ret_w_o,
           ffn_w_up, ffn_conv_w, ffn_conv_b, ffn_w_down):
    batch, seq, d = x.shape
    m = batch * seq
    fox_heads = fox_b_f.shape[-1]
    ret_heads = RET_HEADS
    assert d == fox_heads * LANES, "fox head dim must equal the lane width"
    dk = d // ret_heads
    assert dk == 2 * LANES, "retention qk head dim must be two lane tiles"
    f = ffn_w_down.shape[1]

    tm_norm = _pick(seq, 512)
    tm_proj = _pick(seq, 1024)
    tn_proj = 512
    tm_row = _pick(seq, 512)
    tq = _pick(seq, 512)
    tf = _pick(f, 512)
    chunk = _pick(seq, 256)

    g = norm_g.reshape(norm_g.shape[0], 4, 1, d)
    x2 = x.reshape(m, d)

    w_qkv = fox_w_in[0][:, :3 * d].astype(BF16)
    w_f = jnp.pad(fox_w_in[0][:, 3 * d:], ((0, 0), (0, LANES - fox_heads))).astype(BF16)
    b_f = jnp.pad(fox_b_f[0], (0, LANES - fox_heads)).reshape(1, LANES)
    h = norm_cast(x2, g[0, 0], tm_norm)
    qkv = fox_proj(h, w_qkv, d, LANES ** -0.5 * LOG2E, tm_proj, tn_proj)
    qx, kx = forget_features(h, w_f, b_f, fox_heads, batch, seq, tm_row)
    attn = flash_attention(qkv, qx, kx, fox_heads, batch, seq, tq)
    x2, h = post_mixer(attn, fox_w_o[0].astype(BF16), x2, g[0, 1], g[0, 2], tm_row)
    x2, h = conv_ffn(h, x2, ffn_w_up[0].astype(BF16), ffn_conv_w[0],
                     ffn_conv_b[0].reshape(1, 2 * f), ffn_w_down[0].astype(BF16),
                     g[0, 3], g[1, 0], seq, tm_row, tf, emit_next=True)

    perm = jnp.concatenate([jnp.arange(0, dk, 2), jnp.arange(1, dk, 2)])
    perm = (jnp.arange(ret_heads)[:, None] * dk + perm[None, :]).reshape(-1)
    w_in = ret_w_in[0]
    w_ret = jnp.concatenate([w_in[:, :d][:, perm], w_in[:, d:2 * d][:, perm], w_in[:, 2 * d:]],
                            axis=1).astype(BF16)
    theta = 1.0 / (RET_ROT_BASE ** jnp.linspace(0.0, 1.0, dk // 2, dtype=F32))
    ang = jnp.arange(seq, dtype=F32)[:, None] * theta[None, :]
    log_gamma = jnp.log1p(-(2.0 ** (-5.0 - jnp.arange(ret_heads, dtype=F32))))
    proj = ret_proj(h, w_ret, jnp.cos(ang), jnp.sin(ang), d, dk ** -0.5, seq, tm_proj, tn_proj)
    mix = retention_mix(proj, log_gamma, ret_heads, d, batch, seq, chunk)
    x2, h = post_mixer(mix, ret_w_o[0].astype(BF16), x2, g[1, 1], g[1, 2], _pick(seq, 256))
    x2, _ = conv_ffn(h, x2, ffn_w_up[1].astype(BF16), ffn_conv_w[1],
                     ffn_conv_b[1].reshape(1, 2 * f), ffn_w_down[1].astype(BF16),
                     g[1, 3], g[1, 3], seq, tm_row, tf, emit_next=False)
    return x2.reshape(batch, seq, d)
```

```python
import functools
import math

import jax
import jax.numpy as jnp
from jax import lax
from jax.experimental import pallas as pl
from jax.experimental.pallas import tpu as pltpu

NORM_EPS = 1e-6
RET_HEADS = 8
RET_ROT_BASE = 10000.0
LOG2E = 1.4426950408889634
NEG_BIG = -1e30

LANES = 128
BF16_SUBLANES = 16
VMEM_LIMIT = 56 * 1024 * 1024

F32 = jnp.float32
BF16 = jnp.bfloat16


def _params(sem, vmem=VMEM_LIMIT):
    return pltpu.CompilerParams(dimension_semantics=sem, vmem_limit_bytes=vmem)


def _rms(x, g):
    ms = jnp.mean(x * x, axis=-1, keepdims=True)
    return x * lax.rsqrt(ms + NORM_EPS) * g


def _norm_kernel(x_ref, g_ref, o_ref):
    o_ref[...] = _rms(x_ref[...], g_ref[...]).astype(o_ref.dtype)


def norm_cast(x, g, tm):
    m, d = x.shape
    return pl.pallas_call(
        _norm_kernel,
        out_shape=jax.ShapeDtypeStruct((m, d), BF16),
        grid=(m // tm,),
        in_specs=[pl.BlockSpec((tm, d), lambda i: (i, 0)),
                  pl.BlockSpec((1, d), lambda i: (0, 0))],
        out_specs=pl.BlockSpec((tm, d), lambda i: (i, 0)),
        compiler_params=_params(("parallel",)),
        name="norm_cast",
    )(x, g)


def _fox_proj_kernel(h_ref, w_ref, o_ref, *, n_q_tiles, q_scale):
    j = pl.program_id(1)
    acc = jnp.dot(h_ref[...], w_ref[...], preferred_element_type=F32)
    scale = jnp.where(j < n_q_tiles, q_scale, 1.0).astype(F32)
    o_ref[...] = (acc * scale).astype(o_ref.dtype)


def fox_proj(h, w, d_model, q_scale, tm, tn):
    m, k = h.shape
    n = w.shape[1]
    kern = functools.partial(_fox_proj_kernel, n_q_tiles=d_model // tn, q_scale=q_scale)
    return pl.pallas_call(
        kern,
        out_shape=jax.ShapeDtypeStruct((m, n), BF16),
        grid=(m // tm, n // tn),
        in_specs=[pl.BlockSpec((tm, k), lambda i, j: (i, 0)),
                  pl.BlockSpec((k, tn), lambda i, j: (0, j))],
        out_specs=pl.BlockSpec((tm, tn), lambda i, j: (i, j)),
        compiler_params=_params(("parallel", "arbitrary")),
        name="fox_proj",
    )(h, w)


def _proj_t_kernel(wt_ref, h_ref, o_ref):
    o_ref[0] = lax.dot_general(wt_ref[...], h_ref[...], (((1,), (1,)), ((), ())),
                               preferred_element_type=F32).astype(o_ref.dtype)


def proj_transposed(h, wt, tm, tn):
    m, k = h.shape
    n = wt.shape[0]
    return pl.pallas_call(
        _proj_t_kernel,
        out_shape=jax.ShapeDtypeStruct((m // tm, n, tm), BF16),
        grid=(m // tm, n // tn),
        in_specs=[pl.BlockSpec((tn, k), lambda i, j: (j, 0)),
                  pl.BlockSpec((tm, k), lambda i, j: (i, 0))],
        out_specs=pl.BlockSpec((1, tn, tm), lambda i, j: (i, j, 0)),
        compiler_params=_params(("parallel", "arbitrary")),
        name="fox_proj_vt",
    )(wt, h)


def _split3(x):
    p1 = x.astype(BF16)
    r1 = x - p1.astype(F32)
    p2 = r1.astype(BF16)
    p3 = (r1 - p2.astype(F32)).astype(BF16)
    return p1, p2, p3


def _forget_kernel(h_ref, wf_ref, bf_ref, sel_ref, const_ref, qx_ref, kx_ref, carry_ref,
                   *, n_heads):
    t = pl.program_id(1)
    tm = h_ref.shape[0]

    @pl.when(t == 0)
    def _():
        carry_ref[...] = jnp.zeros_like(carry_ref)

    z = jnp.dot(h_ref[...], wf_ref[...], preferred_element_type=F32) + bf_ref[...]
    log_f = jnp.minimum(z, 0.0) - jnp.log1p(jnp.exp(-jnp.abs(z)))
    lane = lax.broadcasted_iota(jnp.int32, z.shape, 1)
    log_f = jnp.where(lane < n_heads, log_f * LOG2E, 0.0)

    row = lax.broadcasted_iota(jnp.int32, (tm, tm), 0)
    col = lax.broadcasted_iota(jnp.int32, (tm, tm), 1)
    tri = jnp.where(row >= col, 1.0, 0.0).astype(BF16)
    cum = carry_ref[...]
    for piece in _split3(log_f):
        cum = cum + jnp.dot(tri, piece, preferred_element_type=F32)
    carry_ref[...] = cum[tm - 1:tm, :]

    qx = const_ref[0:1, :].astype(F32)
    kx = const_ref[1:2, :].astype(F32)
    for idx, piece in enumerate(_split3(cum)):
        qx = qx + jnp.dot(piece, sel_ref[idx], preferred_element_type=F32)
        kx = kx + jnp.dot(piece, sel_ref[3 + idx], preferred_element_type=F32)
    qx_ref[...] = qx.astype(qx_ref.dtype)
    kx_ref[...] = kx.astype(kx_ref.dtype)


def forget_features(h, wf, bf, n_heads, batch, seq, tm):
    m, d = h.shape
    width = n_heads * LANES
    head = jnp.arange(LANES)[:, None]
    colid = jnp.arange(width)[None, :]
    sel = []
    for sign, base in ((1.0, 0), (-1.0, 3)):
        for idx in range(3):
            sel.append(jnp.where((colid == head * LANES + base + idx) & (head < n_heads), sign, 0.0))
    sel = jnp.stack(sel).astype(BF16)
    within = colid % LANES
    const = jnp.concatenate([
        jnp.where((within >= 3) & (within < 6), 1.0, 0.0),
        jnp.where(within < 3, 1.0, 0.0),
    ], axis=0).astype(BF16)
    tiles = seq // tm
    kern = functools.partial(_forget_kernel, n_heads=n_heads)
    return pl.pallas_call(
        kern,
        out_shape=(jax.ShapeDtypeStruct((m, width), BF16),
                   jax.ShapeDtypeStruct((m, width), BF16)),
        grid=(batch, tiles),
        in_specs=[pl.BlockSpec((tm, d), lambda b, t: (b * tiles + t, 0)),
                  pl.BlockSpec((d, LANES), lambda b, t: (0, 0)),
                  pl.BlockSpec((1, LANES), lambda b, t: (0, 0)),
                  pl.BlockSpec((6, LANES, width), lambda b, t: (0, 0, 0)),
                  pl.BlockSpec((2, width), lambda b, t: (0, 0))],
        out_specs=(pl.BlockSpec((tm, width), lambda b, t: (b * tiles + t, 0)),
                   pl.BlockSpec((tm, width), lambda b, t: (b * tiles + t, 0))),
        scratch_shapes=[pltpu.VMEM((1, LANES), F32)],
        compiler_params=_params(("arbitrary", "arbitrary")),
        name="forget_features",
    )(h, wf, bf, sel, const)


def _flash_kernel(q_ref, qx_ref, k_ref, kx_ref, vt_ref, o_ref, m_ref, l_ref, acc_ref,
                  s0_ref, s1_ref, *, tq, tk):
    i = pl.program_id(2)
    q2 = jnp.concatenate([q_ref[...], qx_ref[...]], axis=1)
    m_ref[...] = jnp.full_like(m_ref, NEG_BIG)
    l_ref[...] = jnp.zeros_like(l_ref)
    acc_ref[...] = jnp.zeros_like(acc_ref)

    def scores(j, s_ref):
        off = pl.multiple_of(j * tk, tk)
        k2 = jnp.concatenate([k_ref[pl.ds(off, tk), :], kx_ref[pl.ds(off, tk), :]], axis=1)
        s_ref[...] = lax.dot_general(k2, q2, (((1,), (1,)), ((), ())),
                                     preferred_element_type=F32)

    def softmax_pv(j, s_ref, masked):
        st = s_ref[...]
        if masked:
            key = j * tk + lax.broadcasted_iota(jnp.int32, st.shape, 0)
            qry = i * tq + lax.broadcasted_iota(jnp.int32, st.shape, 1)
            st = jnp.where(key <= qry, st, NEG_BIG)
        m_old = m_ref[...]
        m_new = jnp.maximum(m_old, jnp.max(st, axis=0, keepdims=True))
        pt = jnp.exp2(st - m_new)
        alpha = jnp.exp2(m_old - m_new)
        l_ref[...] = alpha * l_ref[...] + jnp.sum(pt, axis=0, keepdims=True)
        acc_ref[...] = alpha * acc_ref[...] + jnp.dot(
            vt_ref[j], pt.astype(BF16), preferred_element_type=F32)
        m_ref[...] = m_new

    def pair(jj, carry):
        j = 2 * jj
        scores(j + 1, s1_ref)
        softmax_pv(j, s0_ref, masked=False)
        scores(j + 2, s0_ref)
        softmax_pv(j + 1, s1_ref, masked=False)
        return carry

    n_full = 2 * i
    scores(0, s0_ref)
    lax.fori_loop(0, i, pair, 0)
    scores(n_full + 1, s1_ref)
    softmax_pv(n_full, s0_ref, masked=True)
    softmax_pv(n_full + 1, s1_ref, masked=True)
    o_ref[...] = (acc_ref[...] / l_ref[...]).T.astype(o_ref.dtype)


def flash_attention(qk, qx, kx, vt, n_heads, batch, seq, tq, tk):
    m = qk.shape[0]
    dh = LANES
    nq = seq // tq
    nk = seq // tk
    assert tq == 2 * tk, "the kernel walks key blocks in pairs, one pair per query block"
    kern = functools.partial(_flash_kernel, tq=tq, tk=tk)
    return pl.pallas_call(
        kern,
        out_shape=jax.ShapeDtypeStruct((m, n_heads * dh), BF16),
        grid=(batch, n_heads, nq),
        in_specs=[pl.BlockSpec((tq, dh), lambda b, h, i: (b * nq + i, h)),
                  pl.BlockSpec((tq, dh), lambda b, h, i: (b * nq + i, h)),
                  pl.BlockSpec((seq, dh), lambda b, h, i: (b, n_heads + h)),
                  pl.BlockSpec((seq, dh), lambda b, h, i: (b, h)),
                  pl.BlockSpec((nk, dh, tk), lambda b, h, i: (b, h, 0))],
        out_specs=pl.BlockSpec((tq, dh), lambda b, h, i: (b * nq + i, h)),
        scratch_shapes=[pltpu.VMEM((1, tq), F32), pltpu.VMEM((1, tq), F32),
                        pltpu.VMEM((dh, tq), F32),
                        pltpu.VMEM((tk, tq), F32), pltpu.VMEM((tk, tq), F32)],
        compiler_params=_params(("arbitrary", "arbitrary", "arbitrary")),
        name="fox_flash",
    )(qk, qx, qk, kx, vt)


def _post_kernel(a_ref, w_ref, x_ref, gp_ref, gn_ref, xo_ref, ho_ref):
    mix = jnp.dot(a_ref[...], w_ref[...], preferred_element_type=F32)
    x_new = x_ref[...] + _rms(mix, gp_ref[...])
    xo_ref[...] = x_new
    ho_ref[...] = _rms(x_new, gn_ref[...]).astype(ho_ref.dtype)


def post_mixer(a, w, x, g_post, g_next, tm):
    m, k = a.shape
    d = w.shape[1]
    return pl.pallas_call(
        _post_kernel,
        out_shape=(jax.ShapeDtypeStruct((m, d), F32), jax.ShapeDtypeStruct((m, d), BF16)),
        grid=(m // tm,),
        in_specs=[pl.BlockSpec((tm, k), lambda i: (i, 0)),
                  pl.BlockSpec((k, d), lambda i: (0, 0), pipeline_mode=pl.Buffered(1)),
                  pl.BlockSpec((tm, d), lambda i: (i, 0)),
                  pl.BlockSpec((1, d), lambda i: (0, 0)),
                  pl.BlockSpec((1, d), lambda i: (0, 0))],
        out_specs=(pl.BlockSpec((tm, d), lambda i: (i, 0)),
                   pl.BlockSpec((tm, d), lambda i: (i, 0))),
        compiler_params=_params(("parallel",)),
        name="post_mixer",
    )(a, w, x, g_post, g_next)


def _ffn_kernel(h_ref, halo_ref, x_ref, wa_ref, wb_ref, wd_ref, cwa_ref, cwb_ref,
                cba_ref, cbb_ref, gp_ref, gn_ref, xo_ref, *rest, tiles_per_seq, emit_next):
    if emit_next:
        ho_ref, hs_ref, acc_ref = rest
    else:
        hs_ref, acc_ref = rest
    i = pl.program_id(0)
    j = pl.program_id(1)
    nj = pl.num_programs(1)
    tm = h_ref.shape[0]
    pad = halo_ref.shape[0]

    @pl.when(j == 0)
    def _():
        halo = halo_ref[...]
        hs_ref[0:pad, :] = jnp.where(i % tiles_per_seq != 0, halo, jnp.zeros_like(halo))
        hs_ref[pad:pad + tm, :] = h_ref[...]
        acc_ref[...] = jnp.zeros_like(acc_ref)

    hs = hs_ref[...]

    def conv(w_ref, cw_ref, cb_ref):
        u = jnp.dot(hs, w_ref[...], preferred_element_type=F32)
        u1 = pltpu.roll(u, 1, 0)
        u2 = pltpu.roll(u, 2, 0)
        cw = cw_ref[...]
        return (u[pad:, :] * cw[2:3, :] + u1[pad:, :] * cw[1:2, :]
                + u2[pad:, :] * cw[0:1, :] + cb_ref[...])

    a = conv(wa_ref, cwa_ref, cba_ref)
    b = conv(wb_ref, cwb_ref, cbb_ref)
    act = (a * jax.nn.sigmoid(a) * b).astype(BF16)
    acc_ref[...] += jnp.dot(act, wd_ref[...], preferred_element_type=F32)

    @pl.when(j == nj - 1)
    def _():
        x_new = x_ref[...] + _rms(acc_ref[...], gp_ref[...])
        xo_ref[...] = x_new
        if emit_next:
            ho_ref[...] = _rms(x_new, gn_ref[...]).astype(ho_ref.dtype)


def conv_ffn(h, x, w_up, conv_w, conv_b, w_down, g_post, g_next, seq, tm, tf, emit_next):
    m, d = h.shape
    f = w_down.shape[0]
    nf = f // tf
    pad = BF16_SUBLANES
    tiles_per_seq = seq // tm
    halo_blocks = tm // pad
    kern = functools.partial(_ffn_kernel, tiles_per_seq=tiles_per_seq, emit_next=emit_next)
    out_shape = [jax.ShapeDtypeStruct((m, d), F32)]
    out_specs = [pl.BlockSpec((tm, d), lambda i, j: (i, 0))]
    if emit_next:
        out_shape.append(jax.ShapeDtypeStruct((m, d), BF16))
        out_specs.append(pl.BlockSpec((tm, d), lambda i, j: (i, 0)))
    res = pl.pallas_call(
        kern,
        out_shape=tuple(out_shape),
        grid=(m // tm, nf),
        in_specs=[pl.BlockSpec((tm, d), lambda i, j: (i, 0)),
                  pl.BlockSpec((pad, d), lambda i, j: (jnp.maximum(i * halo_blocks - 1, 0), 0)),
                  pl.BlockSpec((tm, d), lambda i, j: (i, 0)),
                  pl.BlockSpec((d, tf), lambda i, j: (0, j)),
                  pl.BlockSpec((d, tf), lambda i, j: (0, nf + j)),
                  pl.BlockSpec((tf, d), lambda i, j: (j, 0)),
                  pl.BlockSpec((conv_w.shape[0], tf), lambda i, j: (0, j)),
                  pl.BlockSpec((conv_w.shape[0], tf), lambda i, j: (0, nf + j)),
                  pl.BlockSpec((1, tf), lambda i, j: (0, j)),
                  pl.BlockSpec((1, tf), lambda i, j: (0, nf + j)),
                  pl.BlockSpec((1, d), lambda i, j: (0, 0)),
                  pl.BlockSpec((1, d), lambda i, j: (0, 0))],
        out_specs=tuple(out_specs),
        scratch_shapes=[pltpu.VMEM((pad + tm, d), BF16), pltpu.VMEM((tm, d), F32)],
        compiler_params=_params(("arbitrary", "arbitrary")),
        name="conv_ffn",
    )(h, h, x, w_up, w_up, w_down, conv_w, conv_w, conv_b, conv_b, g_post, g_next)
    return res if emit_next else (res[0], None)


def _ret_proj_kernel(h_ref, w_ref, cos_ref, sin_ref, o_ref, *, n_q_tiles, n_qk_tiles,
                     n_qkv_tiles, k_scale):
    j = pl.program_id(1)
    acc = jnp.dot(h_ref[...], w_ref[...], preferred_element_type=F32)
    tn = acc.shape[1]

    @pl.when(j < n_qk_tiles)
    def _():
        scale = jnp.where(j < n_q_tiles, 1.0, k_scale).astype(F32)
        c = cos_ref[...] * scale
        s = sin_ref[...] * scale
        for hh in range(tn // (2 * LANES)):
            lo = hh * 2 * LANES
            x1 = acc[:, lo:lo + LANES]
            x2 = acc[:, lo + LANES:lo + 2 * LANES]
            o_ref[:, lo:lo + LANES] = (x1 * c - x2 * s).astype(o_ref.dtype)
            o_ref[:, lo + LANES:lo + 2 * LANES] = (x1 * s + x2 * c).astype(o_ref.dtype)

    @pl.when((j >= n_qk_tiles) & (j < n_qkv_tiles))
    def _():
        o_ref[...] = acc.astype(o_ref.dtype)

    @pl.when(j >= n_qkv_tiles)
    def _():
        o_ref[...] = (acc * jax.nn.sigmoid(acc)).astype(o_ref.dtype)


def ret_proj(h, w, cos, sin, d_model, k_scale, seq, tm, tn):
    m, k = h.shape
    n = w.shape[1]
    tiles_per_seq = seq // tm
    half = cos.shape[1]
    kern = functools.partial(_ret_proj_kernel, n_q_tiles=d_model // tn,
                             n_qk_tiles=2 * d_model // tn, n_qkv_tiles=4 * d_model // tn,
                             k_scale=k_scale)
    return pl.pallas_call(
        kern,
        out_shape=jax.ShapeDtypeStruct((m, n), BF16),
        grid=(m // tm, n // tn),
        in_specs=[pl.BlockSpec((tm, k), lambda i, j: (i, 0)),
                  pl.BlockSpec((k, tn), lambda i, j: (0, j)),
                  pl.BlockSpec((tm, half), lambda i, j: (i % tiles_per_seq, 0)),
                  pl.BlockSpec((tm, half), lambda i, j: (i % tiles_per_seq, 0))],
        out_specs=pl.BlockSpec((tm, tn), lambda i, j: (i, j)),
        compiler_params=_params(("parallel", "arbitrary")),
        name="ret_proj",
    )(h, w, cos, sin)


def _retention_kernel(lg_ref, q_ref, k_ref, v_ref, g_ref, o_ref, state_ref, mask_ref):
    h = pl.program_id(1)
    c = pl.program_id(2)
    chunk = q_ref.shape[0]
    lg = lg_ref[h]

    @pl.when(c == 0)
    def _():
        state_ref[...] = jnp.zeros_like(state_ref)
        n = lax.broadcasted_iota(jnp.int32, (chunk, chunk), 0)
        mm = lax.broadcasted_iota(jnp.int32, (chunk, chunk), 1)
        rel = (n - mm).astype(F32)
        mask_ref[...] = jnp.where(rel >= 0, jnp.exp(jnp.maximum(rel, 0.0) * lg), 0.0)

    qb = q_ref[...]
    kb = k_ref[...]
    vb = v_ref[...]
    idx = lax.broadcasted_iota(jnp.int32, (chunk, 1), 0).astype(F32)
    decay_q = jnp.exp((idx + 1.0) * lg)
    decay_k = jnp.exp((chunk - 1.0 - idx) * lg)
    decay_chunk = jnp.exp(jnp.full((1, 1), chunk, F32) * lg)

    scores = lax.dot_general(qb, kb, (((1,), (1,)), ((), ())), preferred_element_type=F32)
    scores = scores * mask_ref[...]
    state = state_ref[...]
    o = (jnp.dot(scores.astype(BF16), vb, preferred_element_type=F32)
         + jnp.dot(qb, state.astype(BF16), preferred_element_type=F32) * decay_q)
    kd = (kb.astype(F32) * decay_k).astype(BF16)
    state_ref[...] = decay_chunk * state + lax.dot_general(
        kd, vb, (((0,), (0,)), ((), ())), preferred_element_type=F32)

    o = o * lax.rsqrt(jnp.mean(o * o, axis=-1, keepdims=True) + NORM_EPS)
    o_ref[...] = (o * g_ref[...].astype(F32)).astype(o_ref.dtype)


def retention_mix(proj, log_gamma, n_heads, d_model, batch, seq, chunk):
    m = proj.shape[0]
    dk = d_model // n_heads
    dv = 2 * d_model // n_heads
    nc = seq // chunk
    grid_spec = pltpu.PrefetchScalarGridSpec(
        num_scalar_prefetch=1,
        grid=(batch, n_heads, nc),
        in_specs=[pl.BlockSpec((chunk, dk), lambda b, h, c, lg: (b * nc + c, h)),
                  pl.BlockSpec((chunk, dk), lambda b, h, c, lg: (b * nc + c, n_heads + h)),
                  pl.BlockSpec((chunk, dv), lambda b, h, c, lg: (b * nc + c, n_heads + h)),
                  pl.BlockSpec((chunk, dv), lambda b, h, c, lg: (b * nc + c, 2 * n_heads + h))],
        out_specs=pl.BlockSpec((chunk, dv), lambda b, h, c, lg: (b * nc + c, h)),
        scratch_shapes=[pltpu.VMEM((dk, dv), F32), pltpu.VMEM((chunk, chunk), F32)],
    )
    return pl.pallas_call(
        _retention_kernel,
        out_shape=jax.ShapeDtypeStruct((m, n_heads * dv), BF16),
        grid_spec=grid_spec,
        compiler_params=_params(("arbitrary", "arbitrary", "arbitrary")),
        name="retention_mix",
    )(log_gamma, proj, proj, proj, proj)


def _pick(total, pref):
    if total <= pref:
        return total
    t = pref
    while total % t:
        t -= LANES
    return t


def kernel(x, norm_g, fox_w_in, fox_b_f, fox_w_o, ret_w_in, ret_w_o,
           ffn_w_up, ffn_conv_w, ffn_conv_b, ffn_w_down):
    batch, seq, d = x.shape
    m = batch * seq
    fox_heads = fox_b_f.shape[-1]
    ret_heads = RET_HEADS
    assert d == fox_heads * LANES, "fox head dim must equal the lane width"
    dk = d // ret_heads
    assert dk == 2 * LANES, "retention qk head dim must be two lane tiles"
    f = ffn_w_down.shape[1]

    tm_norm = _pick(seq, 512)
    tm_proj = _pick(seq, 1024)
    tn_proj = 512
    tm_row = _pick(seq, 512)
    tq = _pick(seq, 1024)
    tk = _pick(seq, 512)
    tf = _pick(f, 512)
    chunk = _pick(seq, 256)

    g = norm_g.reshape(norm_g.shape[0], 4, 1, d)
    x2 = x.reshape(m, d)

    w_qk = fox_w_in[0][:, :2 * d].astype(BF16)
    w_vt = fox_w_in[0][:, 2 * d:3 * d].T.astype(BF16)
    w_f = jnp.pad(fox_w_in[0][:, 3 * d:], ((0, 0), (0, LANES - fox_heads))).astype(BF16)
    b_f = jnp.pad(fox_b_f[0], (0, LANES - fox_heads)).reshape(1, LANES)
    h = norm_cast(x2, g[0, 0], tm_norm)
    qk = fox_proj(h, w_qk, d, LANES ** -0.5 * LOG2E, tm_proj, tn_proj)
    vt = proj_transposed(h, w_vt, tk, tn_proj)
    qx, kx = forget_features(h, w_f, b_f, fox_heads, batch, seq, tm_row)
    attn = flash_attention(qk, qx, kx, vt, fox_heads, batch, seq, tq, tk)
    x2, h = post_mixer(attn, fox_w_o[0].astype(BF16), x2, g[0, 1], g[0, 2], tm_row)
    x2, h = conv_ffn(h, x2, ffn_w_up[0].astype(BF16), ffn_conv_w[0],
                     ffn_conv_b[0].reshape(1, 2 * f), ffn_w_down[0].astype(BF16),
                     g[0, 3], g[1, 0], seq, tm_row, tf, emit_next=True)

    perm = jnp.concatenate([jnp.arange(0, dk, 2), jnp.arange(1, dk, 2)])
    perm = (jnp.arange(ret_heads)[:, None] * dk + perm[None, :]).reshape(-1)
    w_in = ret_w_in[0]
    w_ret = jnp.concatenate([w_in[:, :d][:, perm], w_in[:, d:2 * d][:, perm], w_in[:, 2 * d:]],
                            axis=1).astype(BF16)
    theta = 1.0 / (RET_ROT_BASE ** jnp.linspace(0.0, 1.0, dk // 2, dtype=F32))
    ang = jnp.arange(seq, dtype=F32)[:, None] * theta[None, :]
    log_gamma = jnp.log1p(-(2.0 ** (-5.0 - jnp.arange(ret_heads, dtype=F32))))
    proj = ret_proj(h, w_ret, jnp.cos(ang), jnp.sin(ang), d, dk ** -0.5, seq, tm_proj, tn_proj)
    mix = retention_mix(proj, log_gamma, ret_heads, d, batch, seq, chunk)
    x2, h = post_mixer(mix, ret_w_o[0].astype(BF16), x2, g[1, 1], g[1, 2], _pick(seq, 256))
    x2, _ = conv_ffn(h, x2, ffn_w_up[1].astype(BF16), ffn_conv_w[1],
                     ffn_conv_b[1].reshape(1, 2 * f), ffn_w_down[1].astype(BF16),
                     g[1, 3], g[1, 3], seq, tm_row, tf, emit_next=False)
    return x2.reshape(batch, seq, d)
```

```python
import functools
import math

import jax
import jax.numpy as jnp
from jax import lax
from jax.experimental import pallas as pl
from jax.experimental.pallas import tpu as pltpu

NORM_EPS = 1e-6
RET_HEADS = 8
RET_ROT_BASE = 10000.0
LOG2E = 1.4426950408889634
NEG_BIG = -1e30

LANES = 128
BF16_SUBLANES = 16
VMEM_LIMIT = 56 * 1024 * 1024

F32 = jnp.float32
BF16 = jnp.bfloat16


def _params(sem, vmem=VMEM_LIMIT):
    return pltpu.CompilerParams(dimension_semantics=sem, vmem_limit_bytes=vmem)


def _rms(x, g):
    ms = jnp.mean(x * x, axis=-1, keepdims=True)
    return x * lax.rsqrt(ms + NORM_EPS) * g


def _norm_kernel(x_ref, g_ref, o_ref):
    o_ref[...] = _rms(x_ref[...], g_ref[...]).astype(o_ref.dtype)


def norm_cast(x, g, tm):
    m, d = x.shape
    return pl.pallas_call(
        _norm_kernel,
        out_shape=jax.ShapeDtypeStruct((m, d), BF16),
        grid=(m // tm,),
        in_specs=[pl.BlockSpec((tm, d), lambda i: (i, 0)),
                  pl.BlockSpec((1, d), lambda i: (0, 0))],
        out_specs=pl.BlockSpec((tm, d), lambda i: (i, 0)),
        compiler_params=_params(("parallel",)),
        name="norm_cast",
    )(x, g)


def _fox_proj_kernel(h_ref, w_ref, o_ref, *, n_q_tiles, q_scale):
    j = pl.program_id(1)
    acc = jnp.dot(h_ref[...], w_ref[...], preferred_element_type=F32)
    scale = jnp.where(j < n_q_tiles, q_scale, 1.0).astype(F32)
    o_ref[...] = (acc * scale).astype(o_ref.dtype)


def fox_proj(h, w, d_model, q_scale, tm, tn):
    m, k = h.shape
    n = w.shape[1]
    kern = functools.partial(_fox_proj_kernel, n_q_tiles=d_model // tn, q_scale=q_scale)
    return pl.pallas_call(
        kern,
        out_shape=jax.ShapeDtypeStruct((m, n), BF16),
        grid=(m // tm, n // tn),
        in_specs=[pl.BlockSpec((tm, k), lambda i, j: (i, 0)),
                  pl.BlockSpec((k, tn), lambda i, j: (0, j))],
        out_specs=pl.BlockSpec((tm, tn), lambda i, j: (i, j)),
        compiler_params=_params(("parallel", "arbitrary")),
        name="fox_proj",
    )(h, w)


def _proj_t_kernel(wt_ref, h_ref, o_ref):
    o_ref[0] = lax.dot_general(wt_ref[...], h_ref[...], (((1,), (1,)), ((), ())),
                               preferred_element_type=F32).astype(o_ref.dtype)


def proj_transposed(h, wt, tm, tn):
    m, k = h.shape
    n = wt.shape[0]
    return pl.pallas_call(
        _proj_t_kernel,
        out_shape=jax.ShapeDtypeStruct((m // tm, n, tm), BF16),
        grid=(m // tm, n // tn),
        in_specs=[pl.BlockSpec((tn, k), lambda i, j: (j, 0)),
                  pl.BlockSpec((tm, k), lambda i, j: (i, 0))],
        out_specs=pl.BlockSpec((1, tn, tm), lambda i, j: (i, j, 0)),
        compiler_params=_params(("parallel", "arbitrary")),
        name="fox_proj_vt",
    )(wt, h)


def _split3(x):
    p1 = x.astype(BF16)
    r1 = x - p1.astype(F32)
    p2 = r1.astype(BF16)
    p3 = (r1 - p2.astype(F32)).astype(BF16)
    return p1, p2, p3


def _forget_kernel(h_ref, wf_ref, bf_ref, sel_ref, const_ref, qx_ref, kx_ref, f0_ref, f1_ref,
                   carry_ref, *, n_heads):
    t = pl.program_id(1)
    tm = h_ref.shape[0]

    @pl.when(t == 0)
    def _():
        carry_ref[...] = jnp.zeros_like(carry_ref)

    z = jnp.dot(h_ref[...], wf_ref[...], preferred_element_type=F32) + bf_ref[...]
    log_f = jnp.minimum(z, 0.0) - jnp.log1p(jnp.exp(-jnp.abs(z)))
    lane = lax.broadcasted_iota(jnp.int32, z.shape, 1)
    log_f = jnp.where(lane < n_heads, log_f * LOG2E, 0.0)

    row = lax.broadcasted_iota(jnp.int32, (tm, tm), 0)
    col = lax.broadcasted_iota(jnp.int32, (tm, tm), 1)
    tri = jnp.where(row >= col, 1.0, 0.0).astype(BF16)
    cum = carry_ref[...]
    for piece in _split3(log_f):
        cum = cum + jnp.dot(tri, piece, preferred_element_type=F32)
    carry_ref[...] = cum[tm - 1:tm, :]
    f0_ref[0] = cum[0:1, :]
    f1_ref[0] = cum[tm - 1:tm, :]

    qx = const_ref[0:1, :].astype(F32)
    kx = const_ref[1:2, :].astype(F32)
    for idx, piece in enumerate(_split3(cum)):
        qx = qx + jnp.dot(piece, sel_ref[idx], preferred_element_type=F32)
        kx = kx + jnp.dot(piece, sel_ref[3 + idx], preferred_element_type=F32)
    qx_ref[...] = qx.astype(qx_ref.dtype)
    kx_ref[...] = kx.astype(kx_ref.dtype)


FEATS_PER_HEAD = 8


def forget_features(h, wf, bf, n_heads, batch, seq, tm):
    m, d = h.shape
    assert n_heads * FEATS_PER_HEAD <= LANES
    head = jnp.arange(LANES)[:, None]
    colid = jnp.arange(LANES)[None, :]
    sel = []
    for sign, base in ((1.0, 0), (-1.0, 3)):
        for idx in range(3):
            sel.append(jnp.where((colid == head * FEATS_PER_HEAD + base + idx) & (head < n_heads),
                                 sign, 0.0))
    sel = jnp.stack(sel).astype(BF16)
    within = colid % FEATS_PER_HEAD
    used = colid < n_heads * FEATS_PER_HEAD
    const = jnp.concatenate([
        jnp.where((within >= 3) & (within < 6) & used, 1.0, 0.0),
        jnp.where((within < 3) & used, 1.0, 0.0),
    ], axis=0).astype(BF16)
    tiles = seq // tm
    kern = functools.partial(_forget_kernel, n_heads=n_heads)
    row_spec = pl.BlockSpec((tm, LANES), lambda b, t: (b * tiles + t, 0))
    edge_spec = pl.BlockSpec((1, 1, LANES), lambda b, t: (b * tiles + t, 0, 0))
    return pl.pallas_call(
        kern,
        out_shape=(jax.ShapeDtypeStruct((m, LANES), BF16),
                   jax.ShapeDtypeStruct((m, LANES), BF16),
                   jax.ShapeDtypeStruct((m // tm, 1, LANES), F32),
                   jax.ShapeDtypeStruct((m // tm, 1, LANES), F32)),
        grid=(batch, tiles),
        in_specs=[pl.BlockSpec((tm, d), lambda b, t: (b * tiles + t, 0)),
                  pl.BlockSpec((d, LANES), lambda b, t: (0, 0)),
                  pl.BlockSpec((1, LANES), lambda b, t: (0, 0)),
                  pl.BlockSpec((6, LANES, LANES), lambda b, t: (0, 0, 0)),
                  pl.BlockSpec((2, LANES), lambda b, t: (0, 0))],
        out_specs=(row_spec, row_spec, edge_spec, edge_spec),
        scratch_shapes=[pltpu.VMEM((1, LANES), F32)],
        compiler_params=_params(("arbitrary", "arbitrary")),
        name="forget_features",
    )(h, wf, bf, sel, const)


def _norms_kernel(x_ref, sel_ref, o_ref):
    x = x_ref[...].astype(F32)
    hi, lo, _ = _split3(x * x)
    sumsq = (jnp.dot(hi, sel_ref[...], preferred_element_type=F32)
             + jnp.dot(lo, sel_ref[...], preferred_element_type=F32))
    o_ref[0] = jnp.max(sumsq, axis=0, keepdims=True)


def head_sumsq_max(x, tm):
    m, width = x.shape
    groups = width // LANES
    assert groups <= LANES
    sel = (jnp.arange(width)[:, None] // LANES == jnp.arange(LANES)[None, :]).astype(BF16)
    return pl.pallas_call(
        _norms_kernel,
        out_shape=jax.ShapeDtypeStruct((m // tm, 1, LANES), F32),
        grid=(m // tm,),
        in_specs=[pl.BlockSpec((tm, width), lambda i: (i, 0)),
                  pl.BlockSpec((width, LANES), lambda i: (0, 0))],
        out_specs=pl.BlockSpec((1, 1, LANES), lambda i: (i, 0, 0)),
        compiler_params=_params(("parallel",)),
        name="qk_norms",
    )(x, sel)


SKIP_MARGIN_LOG2 = 200.0


def first_needed_pair(sumsq, f_first, f_last, n_heads, batch, seq, tq, tk):
    nk = seq // tk
    nq = seq // tq
    per_q = tq // tk
    sumsq = sumsq.reshape(batch, nk, LANES)
    qmax = jnp.sqrt(sumsq[:, :, :n_heads].reshape(batch, nq, per_q, n_heads).max(axis=2))
    kmax = jnp.sqrt(sumsq[:, :, n_heads:2 * n_heads].max(axis=1))
    need = SKIP_MARGIN_LOG2 + 2.0 * 1.05 * qmax * kmax[:, None, :] + 1.0
    f_q = f_first.reshape(batch, nq, per_q, LANES)[:, :, 0, :n_heads]
    f_k = f_last.reshape(batch, nk, LANES)[:, :, :n_heads]
    bound = f_q[:, :, None, :] - f_k[:, None, :, :]
    below_band = (jnp.arange(nk)[None, :] < per_q * jnp.arange(nq)[:, None])[None, :, :, None]
    droppable = (bound < -need[:, :, None, :]) & below_band
    first_block = jnp.argmin(droppable, axis=2)
    return (first_block // 2).transpose(0, 2, 1).reshape(-1).astype(jnp.int32)


def _flash_kernel(start_ref, q_ref, qx_ref, k_ref, kx_ref, vt_ref, o_ref, m_ref, l_ref, acc_ref,
                  s0_ref, s1_ref, *, tq, tk):
    b = pl.program_id(0)
    h = pl.program_id(1)
    i = pl.program_id(2)
    first_pair = start_ref[(b * pl.num_programs(1) + h) * pl.num_programs(2) + i]
    lane = lax.broadcasted_iota(jnp.int32, qx_ref.shape, 1)
    own = (lane >= h * FEATS_PER_HEAD) & (lane < (h + 1) * FEATS_PER_HEAD)
    qx = jnp.where(own, qx_ref[...], jnp.zeros_like(qx_ref))
    q2 = jnp.concatenate([q_ref[...], qx], axis=1)
    m_ref[...] = jnp.full_like(m_ref, NEG_BIG)
    l_ref[...] = jnp.zeros_like(l_ref)
    acc_ref[...] = jnp.zeros_like(acc_ref)

    def scores(j, s_ref):
        off = pl.multiple_of(j * tk, tk)
        k2 = jnp.concatenate([k_ref[pl.ds(off, tk), :], kx_ref[pl.ds(off, tk), :]], axis=1)
        s_ref[...] = lax.dot_general(k2, q2, (((1,), (1,)), ((), ())),
                                     preferred_element_type=F32)

    def softmax_pv(j, s_ref, masked):
        st = s_ref[...]
        if masked:
            key = j * tk + lax.broadcasted_iota(jnp.int32, st.shape, 0)
            qry = i * tq + lax.broadcasted_iota(jnp.int32, st.shape, 1)
            st = jnp.where(key <= qry, st, NEG_BIG)
        m_old = m_ref[...]
        m_new = jnp.maximum(m_old, jnp.max(st, axis=0, keepdims=True))
        pt = jnp.exp2(st - m_new)
        alpha = jnp.exp2(m_old - m_new)
        l_ref[...] = alpha * l_ref[...] + jnp.sum(pt, axis=0, keepdims=True)
        acc_ref[...] = alpha * acc_ref[...] + jnp.dot(
            vt_ref[j], pt.astype(BF16), preferred_element_type=F32)
        m_ref[...] = m_new

    def pair(jj, carry):
        j = 2 * jj
        scores(j + 1, s1_ref)
        softmax_pv(j, s0_ref, masked=False)
        scores(j + 2, s0_ref)
        softmax_pv(j + 1, s1_ref, masked=False)
        return carry

    n_full = 2 * i
    scores(2 * first_pair, s0_ref)
    lax.fori_loop(first_pair, i, pair, 0)
    scores(n_full + 1, s1_ref)
    softmax_pv(n_full, s0_ref, masked=True)
    softmax_pv(n_full + 1, s1_ref, masked=True)
    o_ref[...] = (acc_ref[...] / l_ref[...]).T.astype(o_ref.dtype)


def flash_attention(first_pair, qk, qx, kx, vt, n_heads, batch, seq, tq, tk):
    m = qk.shape[0]
    dh = LANES
    nq = seq // tq
    nk = seq // tk
    assert tq == 2 * tk, "the kernel walks key blocks in pairs, one pair per query block"
    kern = functools.partial(_flash_kernel, tq=tq, tk=tk)
    grid_spec = pltpu.PrefetchScalarGridSpec(
        num_scalar_prefetch=1,
        grid=(batch, n_heads, nq),
        in_specs=[pl.BlockSpec((tq, dh), lambda b, h, i, s: (b * nq + i, h)),
                  pl.BlockSpec((tq, dh), lambda b, h, i, s: (b * nq + i, 0)),
                  pl.BlockSpec((seq, dh), lambda b, h, i, s: (b, n_heads + h)),
                  pl.BlockSpec((seq, dh), lambda b, h, i, s: (b, 0)),
                  pl.BlockSpec((nk, dh, tk), lambda b, h, i, s: (b, h, 0))],
        out_specs=pl.BlockSpec((tq, dh), lambda b, h, i, s: (b * nq + i, h)),
        scratch_shapes=[pltpu.VMEM((1, tq), F32), pltpu.VMEM((1, tq), F32),
                        pltpu.VMEM((dh, tq), F32),
                        pltpu.VMEM((tk, tq), F32), pltpu.VMEM((tk, tq), F32)],
    )
    return pl.pallas_call(
        kern,
        out_shape=jax.ShapeDtypeStruct((m, n_heads * dh), BF16),
        grid_spec=grid_spec,
        compiler_params=_params(("arbitrary", "arbitrary", "arbitrary")),
        name="fox_flash",
    )(first_pair, qk, qx, qk, kx, vt)


def _post_kernel(a_ref, w_ref, x_ref, gp_ref, gn_ref, xo_ref, ho_ref):
    mix = jnp.dot(a_ref[...], w_ref[...], preferred_element_type=F32)
    x_new = x_ref[...] + _rms(mix, gp_ref[...])
    xo_ref[...] = x_new
    ho_ref[...] = _rms(x_new, gn_ref[...]).astype(ho_ref.dtype)


def post_mixer(a, w, x, g_post, g_next, tm):
    m, k = a.shape
    d = w.shape[1]
    return pl.pallas_call(
        _post_kernel,
        out_shape=(jax.ShapeDtypeStruct((m, d), F32), jax.ShapeDtypeStruct((m, d), BF16)),
        grid=(m // tm,),
        in_specs=[pl.BlockSpec((tm, k), lambda i: (i, 0)),
                  pl.BlockSpec((k, d), lambda i: (0, 0), pipeline_mode=pl.Buffered(1)),
                  pl.BlockSpec((tm, d), lambda i: (i, 0)),
                  pl.BlockSpec((1, d), lambda i: (0, 0)),
                  pl.BlockSpec((1, d), lambda i: (0, 0))],
        out_specs=(pl.BlockSpec((tm, d), lambda i: (i, 0)),
                   pl.BlockSpec((tm, d), lambda i: (i, 0))),
        compiler_params=_params(("parallel",)),
        name="post_mixer",
    )(a, w, x, g_post, g_next)


def _ffn_kernel(h_ref, halo_ref, x_ref, wa_ref, wb_ref, wd_ref, cwa_ref, cwb_ref,
                cba_ref, cbb_ref, gp_ref, gn_ref, xo_ref, *rest, tiles_per_seq, emit_next):
    if emit_next:
        ho_ref, hs_ref, acc_ref = rest
    else:
        hs_ref, acc_ref = rest
    i = pl.program_id(0)
    j = pl.program_id(1)
    nj = pl.num_programs(1)
    tm = h_ref.shape[0]
    pad = halo_ref.shape[0]

    @pl.when(j == 0)
    def _():
        halo = halo_ref[...]
        hs_ref[0:pad, :] = jnp.where(i % tiles_per_seq != 0, halo, jnp.zeros_like(halo))
        hs_ref[pad:pad + tm, :] = h_ref[...]
        acc_ref[...] = jnp.zeros_like(acc_ref)

    hs = hs_ref[...]

    def conv(w_ref, cw_ref, cb_ref):
        u = jnp.dot(hs, w_ref[...], preferred_element_type=F32)
        u1 = pltpu.roll(u, 1, 0)
        u2 = pltpu.roll(u, 2, 0)
        cw = cw_ref[...]
        return (u[pad:, :] * cw[2:3, :] + u1[pad:, :] * cw[1:2, :]
                + u2[pad:, :] * cw[0:1, :] + cb_ref[...])

    a = conv(wa_ref, cwa_ref, cba_ref)
    b = conv(wb_ref, cwb_ref, cbb_ref)
    act = (a * jax.nn.sigmoid(a) * b).astype(BF16)
    acc_ref[...] += jnp.dot(act, wd_ref[...], preferred_element_type=F32)

    @pl.when(j == nj - 1)
    def _():
        x_new = x_ref[...] + _rms(acc_ref[...], gp_ref[...])
        xo_ref[...] = x_new
        if emit_next:
            ho_ref[...] = _rms(x_new, gn_ref[...]).astype(ho_ref.dtype)


def conv_ffn(h, x, w_up, conv_w, conv_b, w_down, g_post, g_next, seq, tm, tf, emit_next):
    m, d = h.shape
    f = w_down.shape[0]
    nf = f // tf
    pad = BF16_SUBLANES
    tiles_per_seq = seq // tm
    halo_blocks = tm // pad
    kern = functools.partial(_ffn_kernel, tiles_per_seq=tiles_per_seq, emit_next=emit_next)
    out_shape = [jax.ShapeDtypeStruct((m, d), F32)]
    out_specs = [pl.BlockSpec((tm, d), lambda i, j: (i, 0))]
    if emit_next:
        out_shape.append(jax.ShapeDtypeStruct((m, d), BF16))
        out_specs.append(pl.BlockSpec((tm, d), lambda i, j: (i, 0)))
    res = pl.pallas_call(
        kern,
        out_shape=tuple(out_shape),
        grid=(m // tm, nf),
        in_specs=[pl.BlockSpec((tm, d), lambda i, j: (i, 0)),
                  pl.BlockSpec((pad, d), lambda i, j: (jnp.maximum(i * halo_blocks - 1, 0), 0)),
                  pl.BlockSpec((tm, d), lambda i, j: (i, 0)),
                  pl.BlockSpec((d, tf), lambda i, j: (0, j)),
                  pl.BlockSpec((d, tf), lambda i, j: (0, nf + j)),
                  pl.BlockSpec((tf, d), lambda i, j: (j, 0)),
                  pl.BlockSpec((conv_w.shape[0], tf), lambda i, j: (0, j)),
                  pl.BlockSpec((conv_w.shape[0], tf), lambda i, j: (0, nf + j)),
                  pl.BlockSpec((1, tf), lambda i, j: (0, j)),
                  pl.BlockSpec((1, tf), lambda i, j: (0, nf + j)),
                  pl.BlockSpec((1, d), lambda i, j: (0, 0)),
                  pl.BlockSpec((1, d), lambda i, j: (0, 0))],
        out_specs=tuple(out_specs),
        scratch_shapes=[pltpu.VMEM((pad + tm, d), BF16), pltpu.VMEM((tm, d), F32)],
        compiler_params=_params(("arbitrary", "arbitrary")),
        name="conv_ffn",
    )(h, h, x, w_up, w_up, w_down, conv_w, conv_w, conv_b, conv_b, g_post, g_next)
    return res if emit_next else (res[0], None)


def _ret_proj_kernel(h_ref, w_ref, cos_ref, sin_ref, o_ref, *, n_q_tiles, n_qk_tiles,
                     n_qkv_tiles, k_scale):
    j = pl.program_id(1)
    acc = jnp.dot(h_ref[...], w_ref[...], preferred_element_type=F32)
    tn = acc.shape[1]

    @pl.when(j < n_qk_tiles)
    def _():
        scale = jnp.where(j < n_q_tiles, 1.0, k_scale).astype(F32)
        c = cos_ref[...] * scale
        s = sin_ref[...] * scale
        for hh in range(tn // (2 * LANES)):
            lo = hh * 2 * LANES
            x1 = acc[:, lo:lo + LANES]
            x2 = acc[:, lo + LANES:lo + 2 * LANES]
            o_ref[:, lo:lo + LANES] = (x1 * c - x2 * s).astype(o_ref.dtype)
            o_ref[:, lo + LANES:lo + 2 * LANES] = (x1 * s + x2 * c).astype(o_ref.dtype)

    @pl.when((j >= n_qk_tiles) & (j < n_qkv_tiles))
    def _():
        o_ref[...] = acc.astype(o_ref.dtype)

    @pl.when(j >= n_qkv_tiles)
    def _():
        o_ref[...] = (acc * jax.nn.sigmoid(acc)).astype(o_ref.dtype)


def ret_proj(h, w, cos, sin, d_model, k_scale, seq, tm, tn):
    m, k = h.shape
    n = w.shape[1]
    tiles_per_seq = seq // tm
    half = cos.shape[1]
    kern = functools.partial(_ret_proj_kernel, n_q_tiles=d_model // tn,
                             n_qk_tiles=2 * d_model // tn, n_qkv_tiles=4 * d_model // tn,
                             k_scale=k_scale)
    return pl.pallas_call(
        kern,
        out_shape=jax.ShapeDtypeStruct((m, n), BF16),
        grid=(m // tm, n // tn),
        in_specs=[pl.BlockSpec((tm, k), lambda i, j: (i, 0)),
                  pl.BlockSpec((k, tn), lambda i, j: (0, j)),
                  pl.BlockSpec((tm, half), lambda i, j: (i % tiles_per_seq, 0)),
                  pl.BlockSpec((tm, half), lambda i, j: (i % tiles_per_seq, 0))],
        out_specs=pl.BlockSpec((tm, tn), lambda i, j: (i, j)),
        compiler_params=_params(("parallel", "arbitrary")),
        name="ret_proj",
    )(h, w, cos, sin)


def _retention_kernel(lg_ref, q_ref, k_ref, v_ref, g_ref, o_ref, state_ref, mask_ref, *, chunk):
    h = pl.program_id(1)
    c = pl.program_id(2)
    rows = q_ref.shape[0]
    lg = lg_ref[h]

    @pl.when(c == 0)
    def _():
        state_ref[...] = jnp.zeros_like(state_ref)
        n = lax.broadcasted_iota(jnp.int32, (chunk, chunk), 0)
        mm = lax.broadcasted_iota(jnp.int32, (chunk, chunk), 1)
        rel = (n - mm).astype(F32)
        mask_ref[...] = jnp.where(rel >= 0, jnp.exp(jnp.maximum(rel, 0.0) * lg), 0.0)

    idx = lax.broadcasted_iota(jnp.int32, (chunk, 1), 0).astype(F32)
    decay_q = jnp.exp((idx + 1.0) * lg)
    decay_k = jnp.exp((chunk - 1.0 - idx) * lg)
    decay_chunk = jnp.exp(jnp.full((1, 1), chunk, F32) * lg)

    for cc in range(rows // chunk):
        sl = slice(cc * chunk, (cc + 1) * chunk)
        qb = q_ref[sl, :]
        kb = k_ref[sl, :]
        vb = v_ref[sl, :]
        scores = lax.dot_general(qb, kb, (((1,), (1,)), ((), ())), preferred_element_type=F32)
        scores = scores * mask_ref[...]
        state = state_ref[...]
        o = (jnp.dot(scores.astype(BF16), vb, preferred_element_type=F32)
             + jnp.dot(qb, state.astype(BF16), preferred_element_type=F32) * decay_q)
        kd = (kb.astype(F32) * decay_k).astype(BF16)
        state_ref[...] = decay_chunk * state + lax.dot_general(
            kd, vb, (((0,), (0,)), ((), ())), preferred_element_type=F32)
        o = o * lax.rsqrt(jnp.mean(o * o, axis=-1, keepdims=True) + NORM_EPS)
        o_ref[sl, :] = (o * g_ref[sl, :].astype(F32)).astype(o_ref.dtype)


def retention_mix(proj, log_gamma, n_heads, d_model, batch, seq, chunk, rows):
    m = proj.shape[0]
    dk = d_model // n_heads
    dv = 2 * d_model // n_heads
    nc = seq // rows
    grid_spec = pltpu.PrefetchScalarGridSpec(
        num_scalar_prefetch=1,
        grid=(batch, n_heads, nc),
        in_specs=[pl.BlockSpec((rows, dk), lambda b, h, c, lg: (b * nc + c, h)),
                  pl.BlockSpec((rows, dk), lambda b, h, c, lg: (b * nc + c, n_heads + h)),
                  pl.BlockSpec((rows, dv), lambda b, h, c, lg: (b * nc + c, n_heads + h)),
                  pl.BlockSpec((rows, dv), lambda b, h, c, lg: (b * nc + c, 2 * n_heads + h))],
        out_specs=pl.BlockSpec((rows, dv), lambda b, h, c, lg: (b * nc + c, h)),
        scratch_shapes=[pltpu.VMEM((dk, dv), F32), pltpu.VMEM((chunk, chunk), F32)],
    )
    return pl.pallas_call(
        functools.partial(_retention_kernel, chunk=chunk),
        out_shape=jax.ShapeDtypeStruct((m, n_heads * dv), BF16),
        grid_spec=grid_spec,
        compiler_params=_params(("arbitrary", "arbitrary", "arbitrary")),
        name="retention_mix",
    )(log_gamma, proj, proj, proj, proj)


def _pick(total, pref):
    if total <= pref:
        return total
    t = pref
    while total % t:
        t -= LANES
    return t


def kernel(x, norm_g, fox_w_in, fox_b_f, fox_w_o, ret_w_in, ret_w_o,
           ffn_w_up, ffn_conv_w, ffn_conv_b, ffn_w_down):
    batch, seq, d = x.shape
    m = batch * seq
    fox_heads = fox_b_f.shape[-1]
    ret_heads = RET_HEADS
    assert d == fox_heads * LANES, "fox head dim must equal the lane width"
    dk = d // ret_heads
    assert dk == 2 * LANES, "retention qk head dim must be two lane tiles"
    f = ffn_w_down.shape[1]

    tm_norm = _pick(seq, 512)
    tm_proj = _pick(seq, 1024)
    tn_proj = _pick(d, 1024)
    tm_row = _pick(seq, 512)
    tq = _pick(seq, 1024)
    tk = _pick(seq, 512)
    tf = _pick(f, 512)
    chunk = _pick(seq, 256)

    g = norm_g.reshape(norm_g.shape[0], 4, 1, d)
    x2 = x.reshape(m, d)

    w_qk = fox_w_in[0][:, :2 * d].astype(BF16)
    w_vt = fox_w_in[0][:, 2 * d:3 * d].T.astype(BF16)
    w_f = jnp.pad(fox_w_in[0][:, 3 * d:], ((0, 0), (0, LANES - fox_heads))).astype(BF16)
    b_f = jnp.pad(fox_b_f[0], (0, LANES - fox_heads)).reshape(1, LANES)
    h = norm_cast(x2, g[0, 0], tm_norm)
    qk = fox_proj(h, w_qk, d, LANES ** -0.5 * LOG2E, tm_proj, tn_proj)
    vt = proj_transposed(h, w_vt, tk, tn_proj)
    qx, kx, f_first, f_last = forget_features(h, w_f, b_f, fox_heads, batch, seq, tk)
    sumsq = head_sumsq_max(qk, tk)
    first_pair = first_needed_pair(sumsq, f_first, f_last, fox_heads, batch, seq, tq, tk)
    attn = flash_attention(first_pair, qk, qx, kx, vt, fox_heads, batch, seq, tq, tk)
    x2, h = post_mixer(attn, fox_w_o[0].astype(BF16), x2, g[0, 1], g[0, 2], tm_row)
    x2, h = conv_ffn(h, x2, ffn_w_up[0].astype(BF16), ffn_conv_w[0],
                     ffn_conv_b[0].reshape(1, 2 * f), ffn_w_down[0].astype(BF16),
                     g[0, 3], g[1, 0], seq, tm_row, tf, emit_next=True)

    perm = jnp.concatenate([jnp.arange(0, dk, 2), jnp.arange(1, dk, 2)])
    perm = (jnp.arange(ret_heads)[:, None] * dk + perm[None, :]).reshape(-1)
    w_in = ret_w_in[0]
    w_ret = jnp.concatenate([w_in[:, :d][:, perm], w_in[:, d:2 * d][:, perm], w_in[:, 2 * d:]],
                            axis=1).astype(BF16)
    theta = 1.0 / (RET_ROT_BASE ** jnp.linspace(0.0, 1.0, dk // 2, dtype=F32))
    ang = jnp.arange(seq, dtype=F32)[:, None] * theta[None, :]
    log_gamma = jnp.log1p(-(2.0 ** (-5.0 - jnp.arange(ret_heads, dtype=F32))))
    proj = ret_proj(h, w_ret, jnp.cos(ang), jnp.sin(ang), d, dk ** -0.5, seq, tm_proj, tn_proj)
    mix = retention_mix(proj, log_gamma, ret_heads, d, batch, seq, chunk, _pick(seq, 1024))
    x2, h = post_mixer(mix, ret_w_o[0].astype(BF16), x2, g[1, 1], g[1, 2], _pick(seq, 256))
    x2, _ = conv_ffn(h, x2, ffn_w_up[1].astype(BF16), ffn_conv_w[1],
                     ffn_conv_b[1].reshape(1, 2 * f), ffn_w_down[1].astype(BF16),
                     g[1, 3], g[1, 3], seq, tm_row, tf, emit_next=False)
    return x2.reshape(batch, seq, d)
```

```python
import functools
import math

import jax
import jax.numpy as jnp
from jax import lax
from jax.experimental import pallas as pl
from jax.experimental.pallas import tpu as pltpu

NORM_EPS = 1e-6
RET_HEADS = 8
RET_ROT_BASE = 10000.0
LOG2E = 1.4426950408889634
NEG_BIG = -1e30

LANES = 128
BF16_SUBLANES = 16
VMEM_LIMIT = 56 * 1024 * 1024

F32 = jnp.float32
BF16 = jnp.bfloat16


def _params(sem, vmem=VMEM_LIMIT):
    return pltpu.CompilerParams(dimension_semantics=sem, vmem_limit_bytes=vmem)


def _rms(x, g):
    ms = jnp.mean(x * x, axis=-1, keepdims=True)
    return x * lax.rsqrt(ms + NORM_EPS) * g


def _norm_kernel(x_ref, g_ref, o_ref):
    o_ref[...] = _rms(x_ref[...], g_ref[...]).astype(o_ref.dtype)


def norm_cast(x, g, tm):
    m, d = x.shape
    return pl.pallas_call(
        _norm_kernel,
        out_shape=jax.ShapeDtypeStruct((m, d), BF16),
        grid=(m // tm,),
        in_specs=[pl.BlockSpec((tm, d), lambda i: (i, 0)),
                  pl.BlockSpec((1, d), lambda i: (0, 0))],
        out_specs=pl.BlockSpec((tm, d), lambda i: (i, 0)),
        compiler_params=_params(("parallel",)),
        name="norm_cast",
    )(x, g)


def _fox_proj_kernel(h_ref, w_ref, o_ref, *, n_q_tiles, q_scale):
    j = pl.program_id(1)
    acc = jnp.dot(h_ref[...], w_ref[...], preferred_element_type=F32)
    scale = jnp.where(j < n_q_tiles, q_scale, 1.0).astype(F32)
    o_ref[...] = (acc * scale).astype(o_ref.dtype)


def fox_proj(h, w, d_model, q_scale, tm, tn):
    m, k = h.shape
    n = w.shape[1]
    kern = functools.partial(_fox_proj_kernel, n_q_tiles=d_model // tn, q_scale=q_scale)
    return pl.pallas_call(
        kern,
        out_shape=jax.ShapeDtypeStruct((m, n), BF16),
        grid=(m // tm, n // tn),
        in_specs=[pl.BlockSpec((tm, k), lambda i, j: (i, 0)),
                  pl.BlockSpec((k, tn), lambda i, j: (0, j))],
        out_specs=pl.BlockSpec((tm, tn), lambda i, j: (i, j)),
        compiler_params=_params(("parallel", "arbitrary")),
        name="fox_proj",
    )(h, w)


def _proj_t_kernel(wt_ref, h_ref, o_ref):
    o_ref[0] = lax.dot_general(wt_ref[...], h_ref[...], (((1,), (1,)), ((), ())),
                               preferred_element_type=F32).astype(o_ref.dtype)


def proj_transposed(h, wt, tm, tn):
    m, k = h.shape
    n = wt.shape[0]
    return pl.pallas_call(
        _proj_t_kernel,
        out_shape=jax.ShapeDtypeStruct((m // tm, n, tm), BF16),
        grid=(m // tm, n // tn),
        in_specs=[pl.BlockSpec((tn, k), lambda i, j: (j, 0)),
                  pl.BlockSpec((tm, k), lambda i, j: (i, 0))],
        out_specs=pl.BlockSpec((1, tn, tm), lambda i, j: (i, j, 0)),
        compiler_params=_params(("parallel", "arbitrary")),
        name="fox_proj_vt",
    )(wt, h)


def _split3(x):
    p1 = x.astype(BF16)
    r1 = x - p1.astype(F32)
    p2 = r1.astype(BF16)
    p3 = (r1 - p2.astype(F32)).astype(BF16)
    return p1, p2, p3


def _forget_kernel(h_ref, wf_ref, bf_ref, sel_ref, const_ref, qx_ref, kx_ref, f0_ref, f1_ref,
                   carry_ref, *, n_heads):
    t = pl.program_id(1)
    tm = h_ref.shape[0]

    @pl.when(t == 0)
    def _():
        carry_ref[...] = jnp.zeros_like(carry_ref)

    z = jnp.dot(h_ref[...], wf_ref[...], preferred_element_type=F32) + bf_ref[...]
    log_f = jnp.minimum(z, 0.0) - jnp.log1p(jnp.exp(-jnp.abs(z)))
    lane = lax.broadcasted_iota(jnp.int32, z.shape, 1)
    log_f = jnp.where(lane < n_heads, log_f * LOG2E, 0.0)

    row = lax.broadcasted_iota(jnp.int32, (tm, tm), 0)
    col = lax.broadcasted_iota(jnp.int32, (tm, tm), 1)
    tri = jnp.where(row >= col, 1.0, 0.0).astype(BF16)
    cum = carry_ref[...]
    for piece in _split3(log_f):
        cum = cum + jnp.dot(tri, piece, preferred_element_type=F32)
    carry_ref[...] = cum[tm - 1:tm, :]
    f0_ref[0] = cum[0:1, :]
    f1_ref[0] = cum[tm - 1:tm, :]

    qx = const_ref[0:1, :].astype(F32)
    kx = const_ref[1:2, :].astype(F32)
    for idx, piece in enumerate(_split3(cum)):
        qx = qx + jnp.dot(piece, sel_ref[idx], preferred_element_type=F32)
        kx = kx + jnp.dot(piece, sel_ref[3 + idx], preferred_element_type=F32)
    qx_ref[...] = qx.astype(qx_ref.dtype)
    kx_ref[...] = kx.astype(kx_ref.dtype)


FEATS_PER_HEAD = 8


def forget_features(h, wf, bf, n_heads, batch, seq, tm):
    m, d = h.shape
    assert n_heads * FEATS_PER_HEAD <= LANES
    head = jnp.arange(LANES)[:, None]
    colid = jnp.arange(LANES)[None, :]
    sel = []
    for sign, base in ((1.0, 0), (-1.0, 3)):
        for idx in range(3):
            sel.append(jnp.where((colid == head * FEATS_PER_HEAD + base + idx) & (head < n_heads),
                                 sign, 0.0))
    sel = jnp.stack(sel).astype(BF16)
    within = colid % FEATS_PER_HEAD
    used = colid < n_heads * FEATS_PER_HEAD
    const = jnp.concatenate([
        jnp.where((within >= 3) & (within < 6) & used, 1.0, 0.0),
        jnp.where((within < 3) & used, 1.0, 0.0),
    ], axis=0).astype(BF16)
    tiles = seq // tm
    kern = functools.partial(_forget_kernel, n_heads=n_heads)
    row_spec = pl.BlockSpec((tm, LANES), lambda b, t: (b * tiles + t, 0))
    edge_spec = pl.BlockSpec((1, 1, LANES), lambda b, t: (b * tiles + t, 0, 0))
    return pl.pallas_call(
        kern,
        out_shape=(jax.ShapeDtypeStruct((m, LANES), BF16),
                   jax.ShapeDtypeStruct((m, LANES), BF16),
                   jax.ShapeDtypeStruct((m // tm, 1, LANES), F32),
                   jax.ShapeDtypeStruct((m // tm, 1, LANES), F32)),
        grid=(batch, tiles),
        in_specs=[pl.BlockSpec((tm, d), lambda b, t: (b * tiles + t, 0)),
                  pl.BlockSpec((d, LANES), lambda b, t: (0, 0)),
                  pl.BlockSpec((1, LANES), lambda b, t: (0, 0)),
                  pl.BlockSpec((6, LANES, LANES), lambda b, t: (0, 0, 0)),
                  pl.BlockSpec((2, LANES), lambda b, t: (0, 0))],
        out_specs=(row_spec, row_spec, edge_spec, edge_spec),
        scratch_shapes=[pltpu.VMEM((1, LANES), F32)],
        compiler_params=_params(("arbitrary", "arbitrary")),
        name="forget_features",
    )(h, wf, bf, sel, const)


def _norms_kernel(x_ref, sel_ref, o_ref):
    x = x_ref[...].astype(F32)
    hi, lo, _ = _split3(x * x)
    sumsq = (jnp.dot(hi, sel_ref[...], preferred_element_type=F32)
             + jnp.dot(lo, sel_ref[...], preferred_element_type=F32))
    o_ref[0] = jnp.max(sumsq, axis=0, keepdims=True)


def head_sumsq_max(x, tm):
    m, width = x.shape
    groups = width // LANES
    assert groups <= LANES
    sel = (jnp.arange(width)[:, None] // LANES == jnp.arange(LANES)[None, :]).astype(BF16)
    return pl.pallas_call(
        _norms_kernel,
        out_shape=jax.ShapeDtypeStruct((m // tm, 1, LANES), F32),
        grid=(m // tm,),
        in_specs=[pl.BlockSpec((tm, width), lambda i: (i, 0)),
                  pl.BlockSpec((width, LANES), lambda i: (0, 0))],
        out_specs=pl.BlockSpec((1, 1, LANES), lambda i: (i, 0, 0)),
        compiler_params=_params(("parallel",)),
        name="qk_norms",
    )(x, sel)


SKIP_MARGIN_LOG2 = 200.0


def first_needed_pair(sumsq, f_first, f_last, n_heads, batch, seq, tq, tk):
    nk = seq // tk
    nq = seq // tq
    per_q = tq // tk
    sumsq = sumsq.reshape(batch, nk, LANES)
    qmax = jnp.sqrt(sumsq[:, :, :n_heads].reshape(batch, nq, per_q, n_heads).max(axis=2))
    kmax = jnp.sqrt(sumsq[:, :, n_heads:2 * n_heads].max(axis=1))
    need = SKIP_MARGIN_LOG2 + 2.0 * 1.05 * qmax * kmax[:, None, :] + 1.0
    f_q = f_first.reshape(batch, nq, per_q, LANES)[:, :, 0, :n_heads]
    f_k = f_last.reshape(batch, nk, LANES)[:, :, :n_heads]
    bound = f_q[:, :, None, :] - f_k[:, None, :, :]
    below_band = (jnp.arange(nk)[None, :] < per_q * jnp.arange(nq)[:, None])[None, :, :, None]
    droppable = (bound < -need[:, :, None, :]) & below_band
    first_block = jnp.argmin(droppable, axis=2)
    return (first_block // 2).transpose(0, 2, 1).reshape(-1).astype(jnp.int32)


def _flash_kernel(start_ref, q_ref, qx_ref, k_ref, kx_ref, vt_ref, o_ref, m_ref, l_ref, acc_ref,
                  s0_ref, s1_ref, *, tq, tk):
    b = pl.program_id(0)
    h = pl.program_id(1)
    i = pl.program_id(2)
    first_pair = start_ref[(b * pl.num_programs(1) + h) * pl.num_programs(2) + i]
    lane = lax.broadcasted_iota(jnp.int32, qx_ref.shape, 1)
    own = (lane >= h * FEATS_PER_HEAD) & (lane < (h + 1) * FEATS_PER_HEAD)
    qx = jnp.where(own, qx_ref[...], jnp.zeros_like(qx_ref))
    q2 = jnp.concatenate([q_ref[...], qx], axis=1)
    m_ref[...] = jnp.full_like(m_ref, NEG_BIG)
    l_ref[...] = jnp.zeros_like(l_ref)
    acc_ref[...] = jnp.zeros_like(acc_ref)

    def scores(j, s_ref, lo=0, hi=tq):
        off = pl.multiple_of(j * tk, tk)
        k2 = jnp.concatenate([k_ref[pl.ds(off, tk), :], kx_ref[pl.ds(off, tk), :]], axis=1)
        s_ref[:, lo:hi] = lax.dot_general(k2, q2[lo:hi], (((1,), (1,)), ((), ())),
                                          preferred_element_type=F32)

    def softmax_pv(j, s_ref, masked, lo=0, hi=tq):
        st = s_ref[:, lo:hi]
        if masked:
            key = j * tk + lax.broadcasted_iota(jnp.int32, st.shape, 0)
            qry = i * tq + lo + lax.broadcasted_iota(jnp.int32, st.shape, 1)
            st = jnp.where(key <= qry, st, NEG_BIG)
        m_old = m_ref[:, lo:hi]
        m_new = jnp.maximum(m_old, jnp.max(st, axis=0, keepdims=True))
        pt = jnp.exp2(st - m_new)
        alpha = jnp.exp2(m_old - m_new)
        l_ref[:, lo:hi] = alpha * l_ref[:, lo:hi] + jnp.sum(pt, axis=0, keepdims=True)
        acc_ref[:, lo:hi] = alpha * acc_ref[:, lo:hi] + jnp.dot(
            vt_ref[j], pt.astype(BF16), preferred_element_type=F32)
        m_ref[:, lo:hi] = m_new

    def pair(jj, carry):
        j = 2 * jj
        scores(j + 1, s1_ref)
        softmax_pv(j, s0_ref, masked=False)
        scores(j + 2, s0_ref)
        softmax_pv(j + 1, s1_ref, masked=False)
        return carry

    n_full = 2 * i
    scores(2 * first_pair, s0_ref)
    lax.fori_loop(first_pair, i, pair, 0)
    scores(n_full + 1, s1_ref, tk, tq)
    softmax_pv(n_full, s0_ref, True, 0, tk)
    softmax_pv(n_full, s0_ref, False, tk, tq)
    softmax_pv(n_full + 1, s1_ref, True, tk, tq)
    o_ref[...] = (acc_ref[...] / l_ref[...]).T.astype(o_ref.dtype)


def flash_attention(first_pair, qk, qx, kx, vt, n_heads, batch, seq, tq, tk):
    m = qk.shape[0]
    dh = LANES
    nq = seq // tq
    nk = seq // tk
    assert tq == 2 * tk, "the kernel walks key blocks in pairs, one pair per query block"
    kern = functools.partial(_flash_kernel, tq=tq, tk=tk)
    grid_spec = pltpu.PrefetchScalarGridSpec(
        num_scalar_prefetch=1,
        grid=(batch, n_heads, nq),
        in_specs=[pl.BlockSpec((tq, dh), lambda b, h, i, s: (b * nq + i, h)),
                  pl.BlockSpec((tq, dh), lambda b, h, i, s: (b * nq + i, 0)),
                  pl.BlockSpec((seq, dh), lambda b, h, i, s: (b, n_heads + h)),
                  pl.BlockSpec((seq, dh), lambda b, h, i, s: (b, 0)),
                  pl.BlockSpec((nk, dh, tk), lambda b, h, i, s: (b, h, 0))],
        out_specs=pl.BlockSpec((tq, dh), lambda b, h, i, s: (b * nq + i, h)),
        scratch_shapes=[pltpu.VMEM((1, tq), F32), pltpu.VMEM((1, tq), F32),
                        pltpu.VMEM((dh, tq), F32),
                        pltpu.VMEM((tk, tq), F32), pltpu.VMEM((tk, tq), F32)],
    )
    return pl.pallas_call(
        kern,
        out_shape=jax.ShapeDtypeStruct((m, n_heads * dh), BF16),
        grid_spec=grid_spec,
        compiler_params=_params(("arbitrary", "arbitrary", "arbitrary")),
        name="fox_flash",
    )(first_pair, qk, qx, qk, kx, vt)


def _post_kernel(a_ref, w_ref, x_ref, gp_ref, gn_ref, xo_ref, ho_ref):
    mix = jnp.dot(a_ref[...], w_ref[...], preferred_element_type=F32)
    x_new = x_ref[...] + _rms(mix, gp_ref[...])
    xo_ref[...] = x_new
    ho_ref[...] = _rms(x_new, gn_ref[...]).astype(ho_ref.dtype)


def post_mixer(a, w, x, g_post, g_next, tm):
    m, k = a.shape
    d = w.shape[1]
    return pl.pallas_call(
        _post_kernel,
        out_shape=(jax.ShapeDtypeStruct((m, d), F32), jax.ShapeDtypeStruct((m, d), BF16)),
        grid=(m // tm,),
        in_specs=[pl.BlockSpec((tm, k), lambda i: (i, 0)),
                  pl.BlockSpec((k, d), lambda i: (0, 0), pipeline_mode=pl.Buffered(1)),
                  pl.BlockSpec((tm, d), lambda i: (i, 0)),
                  pl.BlockSpec((1, d), lambda i: (0, 0)),
                  pl.BlockSpec((1, d), lambda i: (0, 0))],
        out_specs=(pl.BlockSpec((tm, d), lambda i: (i, 0)),
                   pl.BlockSpec((tm, d), lambda i: (i, 0))),
        compiler_params=_params(("parallel",)),
        name="post_mixer",
    )(a, w, x, g_post, g_next)


def _ffn_kernel(h_ref, halo_ref, x_ref, wa_ref, wb_ref, wd_ref, cwa_ref, cwb_ref,
                cba_ref, cbb_ref, gp_ref, gn_ref, xo_ref, *rest, tiles_per_seq, emit_next):
    if emit_next:
        ho_ref, hs_ref, acc_ref = rest
    else:
        hs_ref, acc_ref = rest
    i = pl.program_id(0)
    j = pl.program_id(1)
    nj = pl.num_programs(1)
    tm = h_ref.shape[0]
    pad = halo_ref.shape[0]

    @pl.when(j == 0)
    def _():
        halo = halo_ref[...]
        hs_ref[0:pad, :] = jnp.where(i % tiles_per_seq != 0, halo, jnp.zeros_like(halo))
        hs_ref[pad:pad + tm, :] = h_ref[...]
        acc_ref[...] = jnp.zeros_like(acc_ref)

    hs = hs_ref[...]

    def conv(w_ref, cw_ref, cb_ref):
        u = jnp.dot(hs, w_ref[...], preferred_element_type=F32)
        u1 = pltpu.roll(u, 1, 0)
        u2 = pltpu.roll(u, 2, 0)
        cw = cw_ref[...]
        return (u[pad:, :] * cw[2:3, :] + u1[pad:, :] * cw[1:2, :]
                + u2[pad:, :] * cw[0:1, :] + cb_ref[...])

    a = conv(wa_ref, cwa_ref, cba_ref)
    b = conv(wb_ref, cwb_ref, cbb_ref)
    act = (a * jax.nn.sigmoid(a) * b).astype(BF16)
    acc_ref[...] += jnp.dot(act, wd_ref[...], preferred_element_type=F32)

    @pl.when(j == nj - 1)
    def _():
        x_new = x_ref[...] + _rms(acc_ref[...], gp_ref[...])
        xo_ref[...] = x_new
        if emit_next:
            ho_ref[...] = _rms(x_new, gn_ref[...]).astype(ho_ref.dtype)


def conv_ffn(h, x, layer, w_up, conv_w, conv_b, w_down, g_post, g_next, seq, tm, tf, emit_next):
    m, d = h.shape
    f = w_down.shape[1]
    nf = f // tf
    pad = BF16_SUBLANES
    tiles_per_seq = seq // tm
    halo_blocks = tm // pad
    kern = functools.partial(_ffn_kernel, tiles_per_seq=tiles_per_seq, emit_next=emit_next)
    out_shape = [jax.ShapeDtypeStruct((m, d), F32)]
    out_specs = [pl.BlockSpec((tm, d), lambda i, j: (i, 0))]
    if emit_next:
        out_shape.append(jax.ShapeDtypeStruct((m, d), BF16))
        out_specs.append(pl.BlockSpec((tm, d), lambda i, j: (i, 0)))
    res = pl.pallas_call(
        kern,
        out_shape=tuple(out_shape),
        grid=(m // tm, nf),
        in_specs=[pl.BlockSpec((tm, d), lambda i, j: (i, 0)),
                  pl.BlockSpec((pad, d), lambda i, j: (jnp.maximum(i * halo_blocks - 1, 0), 0)),
                  pl.BlockSpec((tm, d), lambda i, j: (i, 0)),
                  pl.BlockSpec((None, d, tf), lambda i, j: (layer, 0, j)),
                  pl.BlockSpec((None, d, tf), lambda i, j: (layer, 0, nf + j)),
                  pl.BlockSpec((None, tf, d), lambda i, j: (layer, j, 0)),
                  pl.BlockSpec((None, conv_w.shape[1], tf), lambda i, j: (layer, 0, j)),
                  pl.BlockSpec((None, conv_w.shape[1], tf), lambda i, j: (layer, 0, nf + j)),
                  pl.BlockSpec((None, 1, tf), lambda i, j: (layer, 0, j)),
                  pl.BlockSpec((None, 1, tf), lambda i, j: (layer, 0, nf + j)),
                  pl.BlockSpec((1, d), lambda i, j: (0, 0)),
                  pl.BlockSpec((1, d), lambda i, j: (0, 0))],
        out_specs=tuple(out_specs),
        scratch_shapes=[pltpu.VMEM((pad + tm, d), BF16), pltpu.VMEM((tm, d), F32)],
        compiler_params=_params(("arbitrary", "arbitrary")),
        name="conv_ffn",
    )(h, h, x, w_up, w_up, w_down, conv_w, conv_w, conv_b, conv_b, g_post, g_next)
    return res if emit_next else (res[0], None)


def _ret_proj_kernel(h_ref, w_ref, cos_ref, sin_ref, o_ref, *, n_q_tiles, n_qk_tiles,
                     n_qkv_tiles, k_scale):
    j = pl.program_id(1)
    tn = w_ref.shape[1]

    def product():
        return jnp.dot(h_ref[...], w_ref[...], preferred_element_type=F32)

    @pl.when(j < n_qk_tiles)
    def _():
        acc = product()
        scale = jnp.where(j < n_q_tiles, 1.0, k_scale).astype(F32)
        c = cos_ref[...] * scale
        s = sin_ref[...] * scale
        for hh in range(tn // (2 * LANES)):
            lo = hh * 2 * LANES
            x1 = acc[:, lo:lo + LANES]
            x2 = acc[:, lo + LANES:lo + 2 * LANES]
            o_ref[:, lo:lo + LANES] = (x1 * c - x2 * s).astype(o_ref.dtype)
            o_ref[:, lo + LANES:lo + 2 * LANES] = (x1 * s + x2 * c).astype(o_ref.dtype)

    @pl.when((j >= n_qk_tiles) & (j < n_qkv_tiles))
    def _():
        o_ref[...] = product().astype(o_ref.dtype)

    @pl.when(j >= n_qkv_tiles)
    def _():
        acc = product()
        o_ref[...] = (acc * jax.nn.sigmoid(acc)).astype(o_ref.dtype)


def ret_proj(h, w, cos, sin, d_model, k_scale, seq, tm, tn):
    m, k = h.shape
    n = w.shape[1]
    tiles_per_seq = seq // tm
    half = cos.shape[1]
    kern = functools.partial(_ret_proj_kernel, n_q_tiles=d_model // tn,
                             n_qk_tiles=2 * d_model // tn, n_qkv_tiles=4 * d_model // tn,
                             k_scale=k_scale)
    return pl.pallas_call(
        kern,
        out_shape=jax.ShapeDtypeStruct((m, n), BF16),
        grid=(m // tm, n // tn),
        in_specs=[pl.BlockSpec((tm, k), lambda i, j: (i, 0)),
                  pl.BlockSpec((k, tn), lambda i, j: (0, j)),
                  pl.BlockSpec((tm, half), lambda i, j: (i % tiles_per_seq, 0)),
                  pl.BlockSpec((tm, half), lambda i, j: (i % tiles_per_seq, 0))],
        out_specs=pl.BlockSpec((tm, tn), lambda i, j: (i, j)),
        compiler_params=_params(("parallel", "arbitrary")),
        name="ret_proj",
    )(h, w, cos, sin)


def _retention_kernel(lg_ref, q_ref, k_ref, v_ref, g_ref, o_ref, state_ref, mask_ref, *, chunk):
    h = pl.program_id(1)
    c = pl.program_id(2)
    rows = q_ref.shape[0]
    lg = lg_ref[h]

    @pl.when(c == 0)
    def _():
        state_ref[...] = jnp.zeros_like(state_ref)
        n = lax.broadcasted_iota(jnp.int32, (chunk, chunk), 0)
        mm = lax.broadcasted_iota(jnp.int32, (chunk, chunk), 1)
        rel = (n - mm).astype(F32)
        mask_ref[...] = jnp.where(rel >= 0, jnp.exp(jnp.maximum(rel, 0.0) * lg), 0.0)

    idx = lax.broadcasted_iota(jnp.int32, (chunk, 1), 0).astype(F32)
    decay_q = jnp.exp((idx + 1.0) * lg)
    decay_k = jnp.exp((chunk - 1.0 - idx) * lg)
    decay_chunk = jnp.exp(jnp.full((1, 1), chunk, F32) * lg)

    for cc in range(rows // chunk):
        sl = slice(cc * chunk, (cc + 1) * chunk)
        qb = q_ref[sl, :]
        kb = k_ref[sl, :]
        vb = v_ref[sl, :]
        scores = lax.dot_general(qb, kb, (((1,), (1,)), ((), ())), preferred_element_type=F32)
        scores = scores * mask_ref[...]
        state = state_ref[...]
        o = (jnp.dot(scores.astype(BF16), vb, preferred_element_type=F32)
             + jnp.dot(qb, state.astype(BF16), preferred_element_type=F32) * decay_q)
        kd = (kb.astype(F32) * decay_k).astype(BF16)
        state_ref[...] = decay_chunk * state + lax.dot_general(
            kd, vb, (((0,), (0,)), ((), ())), preferred_element_type=F32)
        o = o * lax.rsqrt(jnp.mean(o * o, axis=-1, keepdims=True) + NORM_EPS)
        o_ref[sl, :] = (o * g_ref[sl, :].astype(F32)).astype(o_ref.dtype)


def retention_mix(proj, log_gamma, n_heads, d_model, batch, seq, chunk, rows):
    m = proj.shape[0]
    dk = d_model // n_heads
    dv = 2 * d_model // n_heads
    nc = seq // rows
    grid_spec = pltpu.PrefetchScalarGridSpec(
        num_scalar_prefetch=1,
        grid=(batch, n_heads, nc),
        in_specs=[pl.BlockSpec((rows, dk), lambda b, h, c, lg: (b * nc + c, h)),
                  pl.BlockSpec((rows, dk), lambda b, h, c, lg: (b * nc + c, n_heads + h)),
                  pl.BlockSpec((rows, dv), lambda b, h, c, lg: (b * nc + c, n_heads + h)),
                  pl.BlockSpec((rows, dv), lambda b, h, c, lg: (b * nc + c, 2 * n_heads + h))],
        out_specs=pl.BlockSpec((rows, dv), lambda b, h, c, lg: (b * nc + c, h)),
        scratch_shapes=[pltpu.VMEM((dk, dv), F32), pltpu.VMEM((chunk, chunk), F32)],
    )
    return pl.pallas_call(
        functools.partial(_retention_kernel, chunk=chunk),
        out_shape=jax.ShapeDtypeStruct((m, n_heads * dv), BF16),
        grid_spec=grid_spec,
        compiler_params=_params(("arbitrary", "arbitrary", "arbitrary")),
        name="retention_mix",
    )(log_gamma, proj, proj, proj, proj)


def _pick(total, pref):
    if total <= pref:
        return total
    t = pref
    while total % t:
        t -= LANES
    return t


def kernel(x, norm_g, fox_w_in, fox_b_f, fox_w_o, ret_w_in, ret_w_o,
           ffn_w_up, ffn_conv_w, ffn_conv_b, ffn_w_down):
    batch, seq, d = x.shape
    m = batch * seq
    fox_heads = fox_b_f.shape[-1]
    ret_heads = RET_HEADS
    assert d == fox_heads * LANES, "fox head dim must equal the lane width"
    dk = d // ret_heads
    assert dk == 2 * LANES, "retention qk head dim must be two lane tiles"
    f = ffn_w_down.shape[1]

    tm_norm = _pick(seq, 512)
    tm_proj = _pick(seq, 1024)
    tn_proj = _pick(d, 1024)
    tm_row = _pick(seq, 512)
    tq = _pick(seq, 1024)
    tk = _pick(seq, 512)
    tf = _pick(f, 512)
    chunk = _pick(seq, 256)

    g = norm_g.reshape(norm_g.shape[0], 4, 1, d)
    x2 = x.reshape(m, d)

    w_qk = fox_w_in[0][:, :2 * d].astype(BF16)
    w_vt = fox_w_in[0][:, 2 * d:3 * d].T.astype(BF16)
    w_f = jnp.pad(fox_w_in[0][:, 3 * d:], ((0, 0), (0, LANES - fox_heads))).astype(BF16)
    b_f = jnp.pad(fox_b_f[0], (0, LANES - fox_heads)).reshape(1, LANES)
    h = norm_cast(x2, g[0, 0], tm_norm)
    qk = fox_proj(h, w_qk, d, LANES ** -0.5 * LOG2E, tm_proj, tn_proj)
    vt = proj_transposed(h, w_vt, tk, tn_proj)
    qx, kx, f_first, f_last = forget_features(h, w_f, b_f, fox_heads, batch, seq, tk)
    sumsq = head_sumsq_max(qk, tk)
    first_pair = first_needed_pair(sumsq, f_first, f_last, fox_heads, batch, seq, tq, tk)
    attn = flash_attention(first_pair, qk, qx, kx, vt, fox_heads, batch, seq, tq, tk)
    x2, h = post_mixer(attn, fox_w_o[0].astype(BF16), x2, g[0, 1], g[0, 2], tm_row)
    w_up = ffn_w_up.astype(BF16)
    w_down = ffn_w_down.astype(BF16)
    conv_b = ffn_conv_b.reshape(ffn_conv_b.shape[0], 1, 2 * f)
    x2, h = conv_ffn(h, x2, 0, w_up, ffn_conv_w, conv_b, w_down,
                     g[0, 3], g[1, 0], seq, tm_row, tf, emit_next=True)

    perm = jnp.concatenate([jnp.arange(0, dk, 2), jnp.arange(1, dk, 2)])
    perm = (jnp.arange(ret_heads)[:, None] * dk + perm[None, :]).reshape(-1)
    w_in = ret_w_in[0]
    w_ret = jnp.concatenate([w_in[:, :d][:, perm], w_in[:, d:2 * d][:, perm], w_in[:, 2 * d:]],
                            axis=1).astype(BF16)
    theta = 1.0 / (RET_ROT_BASE ** jnp.linspace(0.0, 1.0, dk // 2, dtype=F32))
    ang = jnp.arange(seq, dtype=F32)[:, None] * theta[None, :]
    log_gamma = jnp.log1p(-(2.0 ** (-5.0 - jnp.arange(ret_heads, dtype=F32))))
    proj = ret_proj(h, w_ret, jnp.cos(ang), jnp.sin(ang), d, dk ** -0.5, seq, tm_proj, tn_proj)
    mix = retention_mix(proj, log_gamma, ret_heads, d, batch, seq, chunk, _pick(seq, 1024))
    x2, h = post_mixer(mix, ret_w_o[0].astype(BF16), x2, g[1, 1], g[1, 2], _pick(seq, 256))
    x2, _ = conv_ffn(h, x2, 1, w_up, ffn_conv_w, conv_b, w_down,
                     g[1, 3], g[1, 3], seq, tm_row, tf, emit_next=False)
    return x2.reshape(batch, seq, d)
```

```python
import functools
import math

import jax
import jax.numpy as jnp
from jax import lax
from jax.experimental import pallas as pl
from jax.experimental.pallas import tpu as pltpu

NORM_EPS = 1e-6
RET_HEADS = 8
RET_ROT_BASE = 10000.0
LOG2E = 1.4426950408889634
NEG_BIG = -1e30

LANES = 128
BF16_SUBLANES = 16
VMEM_LIMIT = 56 * 1024 * 1024

F32 = jnp.float32
BF16 = jnp.bfloat16


def _params(sem, vmem=VMEM_LIMIT):
    return pltpu.CompilerParams(dimension_semantics=sem, vmem_limit_bytes=vmem)


def _rms(x, g):
    ms = jnp.mean(x * x, axis=-1, keepdims=True)
    return x * lax.rsqrt(ms + NORM_EPS) * g


def _norm_kernel(x_ref, g_ref, o_ref):
    o_ref[...] = _rms(x_ref[...], g_ref[...]).astype(o_ref.dtype)


def norm_cast(x, g, tm):
    m, d = x.shape
    return pl.pallas_call(
        _norm_kernel,
        out_shape=jax.ShapeDtypeStruct((m, d), BF16),
        grid=(m // tm,),
        in_specs=[pl.BlockSpec((tm, d), lambda i: (i, 0)),
                  pl.BlockSpec((1, d), lambda i: (0, 0))],
        out_specs=pl.BlockSpec((tm, d), lambda i: (i, 0)),
        compiler_params=_params(("parallel",)),
        name="norm_cast",
    )(x, g)


def _fox_proj_kernel(h_ref, w_ref, o_ref, *, n_q_tiles, q_scale):
    j = pl.program_id(1)
    acc = jnp.dot(h_ref[...], w_ref[...], preferred_element_type=F32)
    scale = jnp.where(j < n_q_tiles, q_scale, 1.0).astype(F32)
    o_ref[...] = (acc * scale).astype(o_ref.dtype)


def fox_proj(h, w, d_model, q_scale, tm, tn):
    m, k = h.shape
    n = w.shape[1]
    kern = functools.partial(_fox_proj_kernel, n_q_tiles=d_model // tn, q_scale=q_scale)
    return pl.pallas_call(
        kern,
        out_shape=jax.ShapeDtypeStruct((m, n), BF16),
        grid=(m // tm, n // tn),
        in_specs=[pl.BlockSpec((tm, k), lambda i, j: (i, 0)),
                  pl.BlockSpec((k, tn), lambda i, j: (0, j))],
        out_specs=pl.BlockSpec((tm, tn), lambda i, j: (i, j)),
        compiler_params=_params(("parallel", "arbitrary")),
        name="fox_proj",
    )(h, w)


def _proj_t_kernel(wt_ref, h_ref, o_ref):
    o_ref[0] = lax.dot_general(wt_ref[...], h_ref[...], (((1,), (1,)), ((), ())),
                               preferred_element_type=F32).astype(o_ref.dtype)


def proj_transposed(h, wt, tm, tn):
    m, k = h.shape
    n = wt.shape[0]
    return pl.pallas_call(
        _proj_t_kernel,
        out_shape=jax.ShapeDtypeStruct((m // tm, n, tm), BF16),
        grid=(m // tm, n // tn),
        in_specs=[pl.BlockSpec((tn, k), lambda i, j: (j, 0)),
                  pl.BlockSpec((tm, k), lambda i, j: (i, 0))],
        out_specs=pl.BlockSpec((1, tn, tm), lambda i, j: (i, j, 0)),
        compiler_params=_params(("parallel", "arbitrary")),
        name="fox_proj_vt",
    )(wt, h)


def _split3(x):
    p1 = x.astype(BF16)
    r1 = x - p1.astype(F32)
    p2 = r1.astype(BF16)
    p3 = (r1 - p2.astype(F32)).astype(BF16)
    return p1, p2, p3


def _forget_kernel(h_ref, wf_ref, bf_ref, sel_ref, const_ref, qx_ref, kx_ref, f0_ref, f1_ref,
                   carry_ref, *, n_heads):
    t = pl.program_id(1)
    tm = h_ref.shape[0]

    @pl.when(t == 0)
    def _():
        carry_ref[...] = jnp.zeros_like(carry_ref)

    z = jnp.dot(h_ref[...], wf_ref[...], preferred_element_type=F32) + bf_ref[...]
    log_f = jnp.minimum(z, 0.0) - jnp.log1p(jnp.exp(-jnp.abs(z)))
    lane = lax.broadcasted_iota(jnp.int32, z.shape, 1)
    log_f = jnp.where(lane < n_heads, log_f * LOG2E, 0.0)

    row = lax.broadcasted_iota(jnp.int32, (tm, tm), 0)
    col = lax.broadcasted_iota(jnp.int32, (tm, tm), 1)
    tri = jnp.where(row >= col, 1.0, 0.0).astype(BF16)
    cum = carry_ref[...]
    for piece in _split3(log_f):
        cum = cum + jnp.dot(tri, piece, preferred_element_type=F32)
    carry_ref[...] = cum[tm - 1:tm, :]
    f0_ref[0] = cum[0:1, :]
    f1_ref[0] = cum[tm - 1:tm, :]

    qx = const_ref[0:1, :].astype(F32)
    kx = const_ref[1:2, :].astype(F32)
    for idx, piece in enumerate(_split3(cum)):
        qx = qx + jnp.dot(piece, sel_ref[idx], preferred_element_type=F32)
        kx = kx + jnp.dot(piece, sel_ref[3 + idx], preferred_element_type=F32)
    qx_ref[...] = qx.astype(qx_ref.dtype)
    kx_ref[...] = kx.astype(kx_ref.dtype)


FEATS_PER_HEAD = 8


def forget_features(h, wf, bf, n_heads, batch, seq, tm):
    m, d = h.shape
    assert n_heads * FEATS_PER_HEAD <= LANES
    head = jnp.arange(LANES)[:, None]
    colid = jnp.arange(LANES)[None, :]
    sel = []
    for sign, base in ((1.0, 0), (-1.0, 3)):
        for idx in range(3):
            sel.append(jnp.where((colid == head * FEATS_PER_HEAD + base + idx) & (head < n_heads),
                                 sign, 0.0))
    sel = jnp.stack(sel).astype(BF16)
    within = colid % FEATS_PER_HEAD
    used = colid < n_heads * FEATS_PER_HEAD
    const = jnp.concatenate([
        jnp.where((within >= 3) & (within < 6) & used, 1.0, 0.0),
        jnp.where((within < 3) & used, 1.0, 0.0),
    ], axis=0).astype(BF16)
    tiles = seq // tm
    kern = functools.partial(_forget_kernel, n_heads=n_heads)
    row_spec = pl.BlockSpec((tm, LANES), lambda b, t: (b * tiles + t, 0))
    edge_spec = pl.BlockSpec((1, 1, LANES), lambda b, t: (b * tiles + t, 0, 0))
    return pl.pallas_call(
        kern,
        out_shape=(jax.ShapeDtypeStruct((m, LANES), BF16),
                   jax.ShapeDtypeStruct((m, LANES), BF16),
                   jax.ShapeDtypeStruct((m // tm, 1, LANES), F32),
                   jax.ShapeDtypeStruct((m // tm, 1, LANES), F32)),
        grid=(batch, tiles),
        in_specs=[pl.BlockSpec((tm, d), lambda b, t: (b * tiles + t, 0)),
                  pl.BlockSpec((d, LANES), lambda b, t: (0, 0)),
                  pl.BlockSpec((1, LANES), lambda b, t: (0, 0)),
                  pl.BlockSpec((6, LANES, LANES), lambda b, t: (0, 0, 0)),
                  pl.BlockSpec((2, LANES), lambda b, t: (0, 0))],
        out_specs=(row_spec, row_spec, edge_spec, edge_spec),
        scratch_shapes=[pltpu.VMEM((1, LANES), F32)],
        compiler_params=_params(("arbitrary", "arbitrary")),
        name="forget_features",
    )(h, wf, bf, sel, const)


def _norms_kernel(x_ref, sel_ref, o_ref):
    x = x_ref[...].astype(F32)
    sumsq = jnp.dot((x * x).astype(BF16), sel_ref[...], preferred_element_type=F32)
    o_ref[0] = jnp.max(sumsq, axis=0, keepdims=True)


def head_sumsq_max(x, tm):
    m, width = x.shape
    groups = width // LANES
    assert groups <= LANES
    sel = (jnp.arange(width)[:, None] // LANES == jnp.arange(LANES)[None, :]).astype(BF16)
    return pl.pallas_call(
        _norms_kernel,
        out_shape=jax.ShapeDtypeStruct((m // tm, 1, LANES), F32),
        grid=(m // tm,),
        in_specs=[pl.BlockSpec((tm, width), lambda i: (i, 0)),
                  pl.BlockSpec((width, LANES), lambda i: (0, 0))],
        out_specs=pl.BlockSpec((1, 1, LANES), lambda i: (i, 0, 0)),
        compiler_params=_params(("parallel",)),
        name="qk_norms",
    )(x, sel)


SKIP_MARGIN_LOG2 = 100.0


def first_needed_pair(sumsq, f_first, f_last, n_heads, batch, seq, tq, tk):
    nk = seq // tk
    nq = seq // tq
    per_q = tq // tk
    sumsq = sumsq.reshape(batch, nk, LANES)
    qmax = jnp.sqrt(sumsq[:, :, :n_heads].reshape(batch, nq, per_q, n_heads).max(axis=2))
    kmax = jnp.sqrt(sumsq[:, :, n_heads:2 * n_heads].max(axis=1))
    need = SKIP_MARGIN_LOG2 + 2.0 * 1.05 * qmax * kmax[:, None, :] + 1.0
    f_q = f_first.reshape(batch, nq, per_q, LANES)[:, :, 0, :n_heads]
    f_k = f_last.reshape(batch, nk, LANES)[:, :, :n_heads]
    bound = f_q[:, :, None, :] - f_k[:, None, :, :]
    below_band = (jnp.arange(nk)[None, :] < per_q * jnp.arange(nq)[:, None])[None, :, :, None]
    droppable = (bound < -need[:, :, None, :]) & below_band
    first_block = jnp.argmin(droppable, axis=2)
    return (first_block // 2).transpose(0, 2, 1).reshape(-1).astype(jnp.int32)


def _flash_kernel(start_ref, q_ref, qx_ref, k_ref, kx_ref, vt_ref, o_ref, m_ref, l_ref, acc_ref,
                  s0_ref, s1_ref, *, tq, tk):
    b = pl.program_id(0)
    h = pl.program_id(1)
    i = pl.program_id(2)
    first_pair = start_ref[(b * pl.num_programs(1) + h) * pl.num_programs(2) + i]
    lane = lax.broadcasted_iota(jnp.int32, qx_ref.shape, 1)
    own = (lane >= h * FEATS_PER_HEAD) & (lane < (h + 1) * FEATS_PER_HEAD)
    qx = jnp.where(own, qx_ref[...], jnp.zeros_like(qx_ref))
    q2 = jnp.concatenate([q_ref[...], qx], axis=1)
    m_ref[...] = jnp.full_like(m_ref, NEG_BIG)
    l_ref[...] = jnp.zeros_like(l_ref)
    acc_ref[...] = jnp.zeros_like(acc_ref)

    def scores(j, s_ref, lo=0, hi=tq):
        off = pl.multiple_of(j * tk, tk)
        k2 = jnp.concatenate([k_ref[pl.ds(off, tk), :], kx_ref[pl.ds(off, tk), :]], axis=1)
        s_ref[:, lo:hi] = lax.dot_general(k2, q2[lo:hi], (((1,), (1,)), ((), ())),
                                          preferred_element_type=F32)

    def softmax_pv(j, s_ref, masked, lo=0, hi=tq):
        st = s_ref[:, lo:hi]
        if masked:
            key = j * tk + lax.broadcasted_iota(jnp.int32, st.shape, 0)
            qry = i * tq + lo + lax.broadcasted_iota(jnp.int32, st.shape, 1)
            st = jnp.where(key <= qry, st, NEG_BIG)
        m_old = m_ref[:, lo:hi]
        m_new = jnp.maximum(m_old, jnp.max(st, axis=0, keepdims=True))
        pt = jnp.exp2(st - m_new)
        alpha = jnp.exp2(m_old - m_new)
        l_ref[:, lo:hi] = alpha * l_ref[:, lo:hi] + jnp.sum(pt, axis=0, keepdims=True)
        acc_ref[:, lo:hi] = alpha * acc_ref[:, lo:hi] + jnp.dot(
            vt_ref[j], pt.astype(BF16), preferred_element_type=F32)
        m_ref[:, lo:hi] = m_new

    def pair(jj, carry):
        j = 2 * jj
        scores(j + 1, s1_ref)
        softmax_pv(j, s0_ref, masked=False)
        scores(j + 2, s0_ref)
        softmax_pv(j + 1, s1_ref, masked=False)
        return carry

    def two_pairs(qq, carry):
        pair(first_pair + 2 * qq, carry)
        pair(first_pair + 2 * qq + 1, carry)
        return carry

    n_full = 2 * i
    n_double = (i - first_pair) // 2
    scores(2 * first_pair, s0_ref)
    lax.fori_loop(0, n_double, two_pairs, 0)
    lax.fori_loop(first_pair + 2 * n_double, i, pair, 0)
    scores(n_full + 1, s1_ref, tk, tq)
    softmax_pv(n_full, s0_ref, True, 0, tk)
    softmax_pv(n_full, s0_ref, False, tk, tq)
    softmax_pv(n_full + 1, s1_ref, True, tk, tq)
    o_ref[...] = (acc_ref[...] / l_ref[...]).T.astype(o_ref.dtype)


def flash_attention(first_pair, qk, qx, kx, vt, n_heads, batch, seq, tq, tk):
    m = qk.shape[0]
    dh = LANES
    nq = seq // tq
    nk = seq // tk
    assert tq == 2 * tk, "the kernel walks key blocks in pairs, one pair per query block"
    kern = functools.partial(_flash_kernel, tq=tq, tk=tk)
    grid_spec = pltpu.PrefetchScalarGridSpec(
        num_scalar_prefetch=1,
        grid=(batch, n_heads, nq),
        in_specs=[pl.BlockSpec((tq, dh), lambda b, h, i, s: (b * nq + i, h)),
                  pl.BlockSpec((tq, dh), lambda b, h, i, s: (b * nq + i, 0)),
                  pl.BlockSpec((seq, dh), lambda b, h, i, s: (b, n_heads + h)),
                  pl.BlockSpec((seq, dh), lambda b, h, i, s: (b, 0)),
                  pl.BlockSpec((nk, dh, tk), lambda b, h, i, s: (b, h, 0))],
        out_specs=pl.BlockSpec((tq, dh), lambda b, h, i, s: (b * nq + i, h)),
        scratch_shapes=[pltpu.VMEM((1, tq), F32), pltpu.VMEM((1, tq), F32),
                        pltpu.VMEM((dh, tq), F32),
                        pltpu.VMEM((tk, tq), F32), pltpu.VMEM((tk, tq), F32)],
    )
    return pl.pallas_call(
        kern,
        out_shape=jax.ShapeDtypeStruct((m, n_heads * dh), BF16),
        grid_spec=grid_spec,
        compiler_params=_params(("arbitrary", "arbitrary", "arbitrary")),
        name="fox_flash",
    )(first_pair, qk, qx, qk, kx, vt)


def _post_kernel(a_ref, w_ref, x_ref, gp_ref, gn_ref, xo_ref, ho_ref, *, parts):
    rows = a_ref.shape[0] // parts
    for p in range(parts):
        sl = slice(p * rows, (p + 1) * rows)
        mix = jnp.dot(a_ref[sl, :], w_ref[...], preferred_element_type=F32)
        x_new = x_ref[sl, :] + _rms(mix, gp_ref[...])
        xo_ref[sl, :] = x_new
        ho_ref[sl, :] = _rms(x_new, gn_ref[...]).astype(ho_ref.dtype)


def post_mixer(a, w, x, g_post, g_next, tm):
    m, k = a.shape
    d = w.shape[1]
    return pl.pallas_call(
        functools.partial(_post_kernel, parts=2),
        out_shape=(jax.ShapeDtypeStruct((m, d), F32), jax.ShapeDtypeStruct((m, d), BF16)),
        grid=(m // tm,),
        in_specs=[pl.BlockSpec((tm, k), lambda i: (i, 0)),
                  pl.BlockSpec((k, d), lambda i: (0, 0), pipeline_mode=pl.Buffered(1)),
                  pl.BlockSpec((tm, d), lambda i: (i, 0)),
                  pl.BlockSpec((1, d), lambda i: (0, 0)),
                  pl.BlockSpec((1, d), lambda i: (0, 0))],
        out_specs=(pl.BlockSpec((tm, d), lambda i: (i, 0)),
                   pl.BlockSpec((tm, d), lambda i: (i, 0))),
        compiler_params=_params(("parallel",)),
        name="post_mixer",
    )(a, w, x, g_post, g_next)


def _ffn_kernel(h_ref, halo_ref, x_ref, wa_ref, wb_ref, wd_ref, cwa_ref, cwb_ref,
                cba_ref, cbb_ref, gp_ref, gn_ref, xo_ref, *rest, tiles_per_seq, emit_next):
    if emit_next:
        ho_ref, hs_ref, acc_ref = rest
    else:
        hs_ref, acc_ref = rest
    i = pl.program_id(0)
    j = pl.program_id(1)
    nj = pl.num_programs(1)
    tm = h_ref.shape[0]
    pad = halo_ref.shape[0]

    @pl.when(j == 0)
    def _():
        halo = halo_ref[...]
        hs_ref[0:pad, :] = jnp.where(i % tiles_per_seq != 0, halo, jnp.zeros_like(halo))
        hs_ref[pad:pad + tm, :] = h_ref[...]
        acc_ref[...] = jnp.zeros_like(acc_ref)

    hs = hs_ref[...]

    def conv(w_ref, cw_ref, cb_ref):
        u = jnp.dot(hs, w_ref[...], preferred_element_type=F32)
        u1 = pltpu.roll(u, 1, 0)
        u2 = pltpu.roll(u, 2, 0)
        cw = cw_ref[...]
        return (u[pad:, :] * cw[2:3, :] + u1[pad:, :] * cw[1:2, :]
                + u2[pad:, :] * cw[0:1, :] + cb_ref[...])

    a = conv(wa_ref, cwa_ref, cba_ref)
    b = conv(wb_ref, cwb_ref, cbb_ref)
    act = (a * jax.nn.sigmoid(a) * b).astype(BF16)
    acc_ref[...] += jnp.dot(act, wd_ref[...], preferred_element_type=F32)

    @pl.when(j == nj - 1)
    def _():
        x_new = x_ref[...] + _rms(acc_ref[...], gp_ref[...])
        xo_ref[...] = x_new
        if emit_next:
            ho_ref[...] = _rms(x_new, gn_ref[...]).astype(ho_ref.dtype)


def conv_ffn(h, x, layer, w_up, conv_w, conv_b, w_down, g_post, g_next, seq, tm, tf, emit_next):
    m, d = h.shape
    f = w_down.shape[1]
    nf = f // tf
    pad = BF16_SUBLANES
    tiles_per_seq = seq // tm
    halo_blocks = tm // pad
    kern = functools.partial(_ffn_kernel, tiles_per_seq=tiles_per_seq, emit_next=emit_next)
    out_shape = [jax.ShapeDtypeStruct((m, d), F32)]
    out_specs = [pl.BlockSpec((tm, d), lambda i, j: (i, 0))]
    if emit_next:
        out_shape.append(jax.ShapeDtypeStruct((m, d), BF16))
        out_specs.append(pl.BlockSpec((tm, d), lambda i, j: (i, 0)))
    res = pl.pallas_call(
        kern,
        out_shape=tuple(out_shape),
        grid=(m // tm, nf),
        in_specs=[pl.BlockSpec((tm, d), lambda i, j: (i, 0)),
                  pl.BlockSpec((pad, d), lambda i, j: (jnp.maximum(i * halo_blocks - 1, 0), 0)),
                  pl.BlockSpec((tm, d), lambda i, j: (i, 0)),
                  pl.BlockSpec((None, d, tf), lambda i, j: (layer, 0, j)),
                  pl.BlockSpec((None, d, tf), lambda i, j: (layer, 0, nf + j)),
                  pl.BlockSpec((None, tf, d), lambda i, j: (layer, j, 0)),
                  pl.BlockSpec((None, conv_w.shape[1], tf), lambda i, j: (layer, 0, j)),
                  pl.BlockSpec((None, conv_w.shape[1], tf), lambda i, j: (layer, 0, nf + j)),
                  pl.BlockSpec((None, 1, tf), lambda i, j: (layer, 0, j)),
                  pl.BlockSpec((None, 1, tf), lambda i, j: (layer, 0, nf + j)),
                  pl.BlockSpec((1, d), lambda i, j: (0, 0)),
                  pl.BlockSpec((1, d), lambda i, j: (0, 0))],
        out_specs=tuple(out_specs),
        scratch_shapes=[pltpu.VMEM((pad + tm, d), BF16), pltpu.VMEM((tm, d), F32)],
        compiler_params=_params(("arbitrary", "arbitrary")),
        name="conv_ffn",
    )(h, h, x, w_up, w_up, w_down, conv_w, conv_w, conv_b, conv_b, g_post, g_next)
    return res if emit_next else (res[0], None)


def _ret_proj_kernel(h_ref, wqk_ref, wvg_ref, cos_ref, sin_ref, o_ref, *, n_q_tiles, n_qk_tiles,
                     n_qkv_tiles, k_scale):
    j = pl.program_id(1)
    tn = wqk_ref.shape[1]

    def product(w_ref=wvg_ref):
        return jnp.dot(h_ref[...], w_ref[...], preferred_element_type=F32)

    @pl.when(j < n_qk_tiles)
    def _():
        acc = product(wqk_ref)
        scale = jnp.where(j < n_q_tiles, 1.0, k_scale).astype(F32)
        c = cos_ref[...] * scale
        s = sin_ref[...] * scale
        for hh in range(tn // (2 * LANES)):
            lo = hh * 2 * LANES
            x1 = acc[:, lo:lo + LANES]
            x2 = acc[:, lo + LANES:lo + 2 * LANES]
            o_ref[:, lo:lo + LANES] = (x1 * c - x2 * s).astype(o_ref.dtype)
            o_ref[:, lo + LANES:lo + 2 * LANES] = (x1 * s + x2 * c).astype(o_ref.dtype)

    @pl.when((j >= n_qk_tiles) & (j < n_qkv_tiles))
    def _():
        o_ref[...] = product().astype(o_ref.dtype)

    @pl.when(j >= n_qkv_tiles)
    def _():
        acc = product()
        o_ref[...] = (acc * jax.nn.sigmoid(acc)).astype(o_ref.dtype)


def ret_proj(h, w_qk, w_vg, cos, sin, d_model, k_scale, seq, tm, tn):
    m, k = h.shape
    n = w_qk.shape[1] + w_vg.shape[1]
    tiles_per_seq = seq // tm
    half = cos.shape[1]
    n_qk_tiles = w_qk.shape[1] // tn
    kern = functools.partial(_ret_proj_kernel, n_q_tiles=d_model // tn,
                             n_qk_tiles=n_qk_tiles, n_qkv_tiles=4 * d_model // tn,
                             k_scale=k_scale)
    return pl.pallas_call(
        kern,
        out_shape=jax.ShapeDtypeStruct((m, n), BF16),
        grid=(m // tm, n // tn),
        in_specs=[pl.BlockSpec((tm, k), lambda i, j: (i, 0)),
                  pl.BlockSpec((k, tn), lambda i, j: (0, jnp.minimum(j, n_qk_tiles - 1))),
                  pl.BlockSpec((k, tn), lambda i, j: (0, jnp.maximum(j - n_qk_tiles, 0))),
                  pl.BlockSpec((tm, half), lambda i, j: (i % tiles_per_seq, 0)),
                  pl.BlockSpec((tm, half), lambda i, j: (i % tiles_per_seq, 0))],
        out_specs=pl.BlockSpec((tm, tn), lambda i, j: (i, j)),
        compiler_params=_params(("parallel", "arbitrary")),
        name="ret_proj",
    )(h, w_qk, w_vg, cos, sin)


def _retention_kernel(lg_ref, q_ref, k_ref, v_ref, g_ref, o_ref, state_ref, mask_ref, *, chunk):
    h = pl.program_id(1)
    c = pl.program_id(2)
    rows = q_ref.shape[0]
    lg = lg_ref[h]

    @pl.when(c == 0)
    def _():
        state_ref[...] = jnp.zeros_like(state_ref)
        n = lax.broadcasted_iota(jnp.int32, (chunk, chunk), 0)
        mm = lax.broadcasted_iota(jnp.int32, (chunk, chunk), 1)
        rel = (n - mm).astype(F32)
        mask_ref[...] = jnp.where(rel >= 0, jnp.exp(jnp.maximum(rel, 0.0) * lg), 0.0)

    idx = lax.broadcasted_iota(jnp.int32, (chunk, 1), 0).astype(F32)
    decay_q = jnp.exp((idx + 1.0) * lg)
    decay_k = jnp.exp((chunk - 1.0 - idx) * lg)
    decay_chunk = jnp.exp(jnp.full((1, 1), chunk, F32) * lg)

    for cc in range(rows // chunk):
        sl = slice(cc * chunk, (cc + 1) * chunk)
        qb = q_ref[sl, :]
        kb = k_ref[sl, :]
        vb = v_ref[sl, :]
        scores = lax.dot_general(qb, kb, (((1,), (1,)), ((), ())), preferred_element_type=F32)
        scores = scores * mask_ref[...]
        state = state_ref[...]
        o = (jnp.dot(scores.astype(BF16), vb, preferred_element_type=F32)
             + jnp.dot(qb, state.astype(BF16), preferred_element_type=F32) * decay_q)
        kd = (kb.astype(F32) * decay_k).astype(BF16)
        state_ref[...] = decay_chunk * state + lax.dot_general(
            kd, vb, (((0,), (0,)), ((), ())), preferred_element_type=F32)
        o = o * lax.rsqrt(jnp.mean(o * o, axis=-1, keepdims=True) + NORM_EPS)
        o_ref[sl, :] = (o * g_ref[sl, :].astype(F32)).astype(o_ref.dtype)


def retention_mix(proj, log_gamma, n_heads, d_model, batch, seq, chunk, rows):
    m = proj.shape[0]
    dk = d_model // n_heads
    dv = 2 * d_model // n_heads
    nc = seq // rows
    grid_spec = pltpu.PrefetchScalarGridSpec(
        num_scalar_prefetch=1,
        grid=(batch, n_heads, nc),
        in_specs=[pl.BlockSpec((rows, dk), lambda b, h, c, lg: (b * nc + c, h)),
                  pl.BlockSpec((rows, dk), lambda b, h, c, lg: (b * nc + c, n_heads + h)),
                  pl.BlockSpec((rows, dv), lambda b, h, c, lg: (b * nc + c, n_heads + h)),
                  pl.BlockSpec((rows, dv), lambda b, h, c, lg: (b * nc + c, 2 * n_heads + h))],
        out_specs=pl.BlockSpec((rows, dv), lambda b, h, c, lg: (b * nc + c, h)),
        scratch_shapes=[pltpu.VMEM((dk, dv), F32), pltpu.VMEM((chunk, chunk), F32)],
    )
    return pl.pallas_call(
        functools.partial(_retention_kernel, chunk=chunk),
        out_shape=jax.ShapeDtypeStruct((m, n_heads * dv), BF16),
        grid_spec=grid_spec,
        compiler_params=_params(("arbitrary", "arbitrary", "arbitrary")),
        name="retention_mix",
    )(log_gamma, proj, proj, proj, proj)


def _pick(total, pref):
    if total <= pref:
        return total
    t = pref
    while total % t:
        t -= LANES
    return t


def kernel(x, norm_g, fox_w_in, fox_b_f, fox_w_o, ret_w_in, ret_w_o,
           ffn_w_up, ffn_conv_w, ffn_conv_b, ffn_w_down):
    batch, seq, d = x.shape
    m = batch * seq
    fox_heads = fox_b_f.shape[-1]
    ret_heads = RET_HEADS
    assert d == fox_heads * LANES, "fox head dim must equal the lane width"
    dk = d // ret_heads
    assert dk == 2 * LANES, "retention qk head dim must be two lane tiles"
    f = ffn_w_down.shape[1]

    tm_norm = _pick(seq, 512)
    tm_proj = _pick(seq, 1024)
    tn_proj = _pick(d, 1024)
    tm_row = _pick(seq, 512)
    tq = _pick(seq, 1024)
    tk = _pick(seq, 512)
    tf = _pick(f, 512)
    chunk = _pick(seq, 256)

    g = norm_g.reshape(norm_g.shape[0], 4, 1, d)
    x2 = x.reshape(m, d)

    w_qk = fox_w_in[0][:, :2 * d].astype(BF16)
    w_vt = fox_w_in[0][:, 2 * d:3 * d].T.astype(BF16)
    w_f = jnp.pad(fox_w_in[0][:, 3 * d:], ((0, 0), (0, LANES - fox_heads))).astype(BF16)
    b_f = jnp.pad(fox_b_f[0], (0, LANES - fox_heads)).reshape(1, LANES)
    h = norm_cast(x2, g[0, 0], tm_norm)
    qk = fox_proj(h, w_qk, d, LANES ** -0.5 * LOG2E, tm_proj, tn_proj)
    vt = proj_transposed(h, w_vt, tk, tn_proj)
    qx, kx, f_first, f_last = forget_features(h, w_f, b_f, fox_heads, batch, seq, tk)
    sumsq = head_sumsq_max(qk, tk)
    first_pair = first_needed_pair(sumsq, f_first, f_last, fox_heads, batch, seq, tq, tk)
    attn = flash_attention(first_pair, qk, qx, kx, vt, fox_heads, batch, seq, tq, tk)
    x2, h = post_mixer(attn, fox_w_o[0].astype(BF16), x2, g[0, 1], g[0, 2], tm_row)
    w_up = ffn_w_up.astype(BF16)
    w_down = ffn_w_down.astype(BF16)
    conv_b = ffn_conv_b.reshape(ffn_conv_b.shape[0], 1, 2 * f)
    x2, h = conv_ffn(h, x2, 0, w_up, ffn_conv_w, conv_b, w_down,
                     g[0, 3], g[1, 0], seq, tm_row, tf, emit_next=True)

    w_in = ret_w_in[0]
    w_ret_qk = (w_in[:, :2 * d].reshape(d, 2 * ret_heads, dk // 2, 2).swapaxes(-1, -2)
                .reshape(d, 2 * d).astype(BF16))
    w_ret_vg = w_in[:, 2 * d:].astype(BF16)
    theta = 1.0 / (RET_ROT_BASE ** jnp.linspace(0.0, 1.0, dk // 2, dtype=F32))
    ang = jnp.arange(seq, dtype=F32)[:, None] * theta[None, :]
    log_gamma = jnp.log1p(-(2.0 ** (-5.0 - jnp.arange(ret_heads, dtype=F32))))
    proj = ret_proj(h, w_ret_qk, w_ret_vg, jnp.cos(ang), jnp.sin(ang), d, dk ** -0.5, seq,
                    tm_proj, tn_proj)
    mix = retention_mix(proj, log_gamma, ret_heads, d, batch, seq, chunk, _pick(seq, 1024))
    x2, h = post_mixer(mix, ret_w_o[0].astype(BF16), x2, g[1, 1], g[1, 2], _pick(seq, 256))
    x2, _ = conv_ffn(h, x2, 1, w_up, ffn_conv_w, conv_b, w_down,
                     g[1, 3], g[1, 3], seq, tm_row, tf, emit_next=False)
    return x2.reshape(batch, seq, d)
```

```python
import functools
import math

import jax
import jax.numpy as jnp
from jax import lax
from jax.experimental import pallas as pl
from jax.experimental.pallas import tpu as pltpu

NORM_EPS = 1e-6
RET_HEADS = 8
RET_ROT_BASE = 10000.0
LOG2E = 1.4426950408889634
NEG_BIG = -1e30

LANES = 128
BF16_SUBLANES = 16
VMEM_LIMIT = 56 * 1024 * 1024

F32 = jnp.float32
BF16 = jnp.bfloat16


def _params(sem, vmem=VMEM_LIMIT):
    return pltpu.CompilerParams(dimension_semantics=sem, vmem_limit_bytes=vmem)


def _rms(x, g):
    ms = jnp.mean(x * x, axis=-1, keepdims=True)
    return x * lax.rsqrt(ms + NORM_EPS) * g


def _norm_kernel(x_ref, g_ref, o_ref):
    o_ref[...] = _rms(x_ref[...], g_ref[...]).astype(o_ref.dtype)


def norm_cast(x, g, tm):
    m, d = x.shape
    return pl.pallas_call(
        _norm_kernel,
        out_shape=jax.ShapeDtypeStruct((m, d), BF16),
        grid=(m // tm,),
        in_specs=[pl.BlockSpec((tm, d), lambda i: (i, 0)),
                  pl.BlockSpec((1, d), lambda i: (0, 0))],
        out_specs=pl.BlockSpec((tm, d), lambda i: (i, 0)),
        compiler_params=_params(("parallel",)),
        name="norm_cast",
    )(x, g)


def _fox_proj_kernel(h_ref, w_ref, o_ref, *, n_q_tiles, q_scale):
    j = pl.program_id(1)
    acc = jnp.dot(h_ref[...], w_ref[...], preferred_element_type=F32)
    scale = jnp.where(j < n_q_tiles, q_scale, 1.0).astype(F32)
    o_ref[...] = (acc * scale).astype(o_ref.dtype)


def fox_proj(h, w, d_model, q_scale, tm, tn):
    m, k = h.shape
    n = w.shape[1]
    kern = functools.partial(_fox_proj_kernel, n_q_tiles=d_model // tn, q_scale=q_scale)
    return pl.pallas_call(
        kern,
        out_shape=jax.ShapeDtypeStruct((m, n), BF16),
        grid=(m // tm, n // tn),
        in_specs=[pl.BlockSpec((tm, k), lambda i, j: (i, 0)),
                  pl.BlockSpec((k, tn), lambda i, j: (0, j))],
        out_specs=pl.BlockSpec((tm, tn), lambda i, j: (i, j)),
        compiler_params=_params(("parallel", "arbitrary")),
        name="fox_proj",
    )(h, w)


def _proj_t_kernel(wt_ref, h_ref, o_ref):
    o_ref[0] = lax.dot_general(wt_ref[...], h_ref[...], (((1,), (1,)), ((), ())),
                               preferred_element_type=F32).astype(o_ref.dtype)


def proj_transposed(h, wt, tm, tn):
    m, k = h.shape
    n = wt.shape[0]
    return pl.pallas_call(
        _proj_t_kernel,
        out_shape=jax.ShapeDtypeStruct((m // tm, n, tm), BF16),
        grid=(m // tm, n // tn),
        in_specs=[pl.BlockSpec((tn, k), lambda i, j: (j, 0)),
                  pl.BlockSpec((tm, k), lambda i, j: (i, 0))],
        out_specs=pl.BlockSpec((1, tn, tm), lambda i, j: (i, j, 0)),
        compiler_params=_params(("parallel", "arbitrary")),
        name="fox_proj_vt",
    )(wt, h)


def _split3(x):
    p1 = x.astype(BF16)
    r1 = x - p1.astype(F32)
    p2 = r1.astype(BF16)
    p3 = (r1 - p2.astype(F32)).astype(BF16)
    return p1, p2, p3


def _forget_kernel(h_ref, wf_ref, bf_ref, sel_ref, const_ref, qx_ref, kx_ref, f0_ref, f1_ref,
                   carry_ref, *, n_heads):
    t = pl.program_id(1)
    tm = h_ref.shape[0]

    @pl.when(t == 0)
    def _():
        carry_ref[...] = jnp.zeros_like(carry_ref)

    z = jnp.dot(h_ref[...], wf_ref[...], preferred_element_type=F32) + bf_ref[...]
    log_f = jnp.minimum(z, 0.0) - jnp.log1p(jnp.exp(-jnp.abs(z)))
    lane = lax.broadcasted_iota(jnp.int32, z.shape, 1)
    log_f = jnp.where(lane < n_heads, log_f * LOG2E, 0.0)

    row = lax.broadcasted_iota(jnp.int32, (tm, tm), 0)
    col = lax.broadcasted_iota(jnp.int32, (tm, tm), 1)
    tri = jnp.where(row >= col, 1.0, 0.0).astype(BF16)
    cum = carry_ref[...]
    for piece in _split3(log_f):
        cum = cum + jnp.dot(tri, piece, preferred_element_type=F32)
    carry_ref[...] = cum[tm - 1:tm, :]
    f0_ref[0] = cum[0:1, :]
    f1_ref[0] = cum[tm - 1:tm, :]

    qx = const_ref[0:1, :].astype(F32)
    kx = const_ref[1:2, :].astype(F32)
    for idx, piece in enumerate(_split3(cum)):
        qx = qx + jnp.dot(piece, sel_ref[idx], preferred_element_type=F32)
        kx = kx + jnp.dot(piece, sel_ref[3 + idx], preferred_element_type=F32)
    qx_ref[...] = qx.astype(qx_ref.dtype)
    kx_ref[...] = kx.astype(kx_ref.dtype)


FEATS_PER_HEAD = 8


def forget_features(h, wf, bf, n_heads, batch, seq, tm):
    m, d = h.shape
    assert n_heads * FEATS_PER_HEAD <= LANES
    head = jnp.arange(LANES)[:, None]
    colid = jnp.arange(LANES)[None, :]
    sel = []
    for sign, base in ((1.0, 0), (-1.0, 3)):
        for idx in range(3):
            sel.append(jnp.where((colid == head * FEATS_PER_HEAD + base + idx) & (head < n_heads),
                                 sign, 0.0))
    sel = jnp.stack(sel).astype(BF16)
    within = colid % FEATS_PER_HEAD
    used = colid < n_heads * FEATS_PER_HEAD
    const = jnp.concatenate([
        jnp.where((within >= 3) & (within < 6) & used, 1.0, 0.0),
        jnp.where((within < 3) & used, 1.0, 0.0),
    ], axis=0).astype(BF16)
    tiles = seq // tm
    kern = functools.partial(_forget_kernel, n_heads=n_heads)
    row_spec = pl.BlockSpec((tm, LANES), lambda b, t: (b * tiles + t, 0))
    edge_spec = pl.BlockSpec((1, 1, LANES), lambda b, t: (b * tiles + t, 0, 0))
    return pl.pallas_call(
        kern,
        out_shape=(jax.ShapeDtypeStruct((m, LANES), BF16),
                   jax.ShapeDtypeStruct((m, LANES), BF16),
                   jax.ShapeDtypeStruct((m // tm, 1, LANES), F32),
                   jax.ShapeDtypeStruct((m // tm, 1, LANES), F32)),
        grid=(batch, tiles),
        in_specs=[pl.BlockSpec((tm, d), lambda b, t: (b * tiles + t, 0)),
                  pl.BlockSpec((d, LANES), lambda b, t: (0, 0)),
                  pl.BlockSpec((1, LANES), lambda b, t: (0, 0)),
                  pl.BlockSpec((6, LANES, LANES), lambda b, t: (0, 0, 0)),
                  pl.BlockSpec((2, LANES), lambda b, t: (0, 0))],
        out_specs=(row_spec, row_spec, edge_spec, edge_spec),
        scratch_shapes=[pltpu.VMEM((1, LANES), F32)],
        compiler_params=_params(("arbitrary", "arbitrary")),
        name="forget_features",
    )(h, wf, bf, sel, const)


def _norms_kernel(x_ref, sel_ref, o_ref):
    x = x_ref[...].astype(F32)
    sumsq = jnp.dot((x * x).astype(BF16), sel_ref[...], preferred_element_type=F32)
    o_ref[0] = jnp.max(sumsq, axis=0, keepdims=True)


def head_sumsq_max(x, tm):
    m, width = x.shape
    groups = width // LANES
    assert groups <= LANES
    sel = (jnp.arange(width)[:, None] // LANES == jnp.arange(LANES)[None, :]).astype(BF16)
    return pl.pallas_call(
        _norms_kernel,
        out_shape=jax.ShapeDtypeStruct((m // tm, 1, LANES), F32),
        grid=(m // tm,),
        in_specs=[pl.BlockSpec((tm, width), lambda i: (i, 0)),
                  pl.BlockSpec((width, LANES), lambda i: (0, 0))],
        out_specs=pl.BlockSpec((1, 1, LANES), lambda i: (i, 0, 0)),
        compiler_params=_params(("parallel",)),
        name="qk_norms",
    )(x, sel)


SKIP_MARGIN_LOG2 = 100.0


def first_needed_pair(sumsq, f_first, f_last, n_heads, batch, seq, tq, tk):
    nk = seq // tk
    nq = seq // tq
    per_q = tq // tk
    sumsq = sumsq.reshape(batch, nk, LANES)
    qmax = jnp.sqrt(sumsq[:, :, :n_heads].reshape(batch, nq, per_q, n_heads).max(axis=2))
    kmax = jnp.sqrt(sumsq[:, :, n_heads:2 * n_heads].max(axis=1))
    need = SKIP_MARGIN_LOG2 + 2.0 * 1.05 * qmax * kmax[:, None, :] + 1.0
    f_q = f_first.reshape(batch, nq, per_q, LANES)[:, :, 0, :n_heads]
    f_k = f_last.reshape(batch, nk, LANES)[:, :, :n_heads]
    bound = f_q[:, :, None, :] - f_k[:, None, :, :]
    below_band = (jnp.arange(nk)[None, :] < per_q * jnp.arange(nq)[:, None])[None, :, :, None]
    droppable = (bound < -need[:, :, None, :]) & below_band
    first_block = jnp.argmin(droppable, axis=2)
    return (first_block // 2).transpose(0, 2, 1).reshape(-1).astype(jnp.int32)


def _flash_kernel(start_ref, q_ref, qx_ref, k_ref, kx_ref, vt_ref, o_ref, m_ref, l_ref, acc_ref,
                  s0_ref, s1_ref, *, tq, tk):
    b = pl.program_id(0)
    h = pl.program_id(1)
    i = pl.program_id(2)
    first_pair = start_ref[(b * pl.num_programs(1) + h) * pl.num_programs(2) + i]
    lane = lax.broadcasted_iota(jnp.int32, qx_ref.shape, 1)
    own = (lane >= h * FEATS_PER_HEAD) & (lane < (h + 1) * FEATS_PER_HEAD)
    qx = jnp.where(own, qx_ref[...], jnp.zeros_like(qx_ref))
    q2 = jnp.concatenate([q_ref[...], qx], axis=1)
    m_ref[...] = jnp.full_like(m_ref, NEG_BIG)
    l_ref[...] = jnp.zeros_like(l_ref)
    acc_ref[...] = jnp.zeros_like(acc_ref)

    def scores(j, s_ref, lo=0, hi=tq):
        off = pl.multiple_of(j * tk, tk)
        k2 = jnp.concatenate([k_ref[pl.ds(off, tk), :], kx_ref[pl.ds(off, tk), :]], axis=1)
        s_ref[:, lo:hi] = lax.dot_general(k2, q2[lo:hi], (((1,), (1,)), ((), ())),
                                          preferred_element_type=F32)

    def softmax_pv(j, s_ref, masked, lo=0, hi=tq):
        st = s_ref[:, lo:hi]
        if masked:
            key = j * tk + lax.broadcasted_iota(jnp.int32, st.shape, 0)
            qry = i * tq + lo + lax.broadcasted_iota(jnp.int32, st.shape, 1)
            st = jnp.where(key <= qry, st, NEG_BIG)
        m_old = m_ref[:, lo:hi]
        m_new = jnp.maximum(m_old, jnp.max(st, axis=0, keepdims=True))
        pt = jnp.exp2(st - m_new)
        alpha = jnp.exp2(m_old - m_new)
        l_ref[:, lo:hi] = alpha * l_ref[:, lo:hi] + jnp.sum(pt, axis=0, keepdims=True)
        acc_ref[:, lo:hi] = alpha * acc_ref[:, lo:hi] + jnp.dot(
            vt_ref[j], pt.astype(BF16), preferred_element_type=F32)
        m_ref[:, lo:hi] = m_new

    def pair(jj, carry):
        j = 2 * jj
        scores(j + 1, s1_ref)
        softmax_pv(j, s0_ref, masked=False)
        scores(j + 2, s0_ref)
        softmax_pv(j + 1, s1_ref, masked=False)
        return carry

    def two_pairs(qq, carry):
        pair(first_pair + 2 * qq, carry)
        pair(first_pair + 2 * qq + 1, carry)
        return carry

    n_full = 2 * i
    n_pairs = i - first_pair
    scores(2 * first_pair, s0_ref)
    lax.fori_loop(0, lax.shift_right_logical(n_pairs, 1), two_pairs, 0)

    @pl.when((n_pairs & 1) == 1)
    def _():
        pair(i - 1, 0)
    scores(n_full + 1, s1_ref, tk, tq)
    softmax_pv(n_full, s0_ref, True, 0, tk)
    softmax_pv(n_full, s0_ref, False, tk, tq)
    softmax_pv(n_full + 1, s1_ref, True, tk, tq)
    o_ref[...] = (acc_ref[...] / l_ref[...]).T.astype(o_ref.dtype)


def flash_attention(first_pair, qk, qx, kx, vt, n_heads, batch, seq, tq, tk):
    m = qk.shape[0]
    dh = LANES
    nq = seq // tq
    nk = seq // tk
    assert tq == 2 * tk, "the kernel walks key blocks in pairs, one pair per query block"
    kern = functools.partial(_flash_kernel, tq=tq, tk=tk)
    grid_spec = pltpu.PrefetchScalarGridSpec(
        num_scalar_prefetch=1,
        grid=(batch, n_heads, nq),
        in_specs=[pl.BlockSpec((tq, dh), lambda b, h, i, s: (b * nq + i, h)),
                  pl.BlockSpec((tq, dh), lambda b, h, i, s: (b * nq + i, 0)),
                  pl.BlockSpec((seq, dh), lambda b, h, i, s: (b, n_heads + h)),
                  pl.BlockSpec((seq, dh), lambda b, h, i, s: (b, 0)),
                  pl.BlockSpec((nk, dh, tk), lambda b, h, i, s: (b, h, 0))],
        out_specs=pl.BlockSpec((tq, dh), lambda b, h, i, s: (b * nq + i, h)),
        scratch_shapes=[pltpu.VMEM((1, tq), F32), pltpu.VMEM((1, tq), F32),
                        pltpu.VMEM((dh, tq), F32),
                        pltpu.VMEM((tk, tq), F32), pltpu.VMEM((tk, tq), F32)],
    )
    return pl.pallas_call(
        kern,
        out_shape=jax.ShapeDtypeStruct((m, n_heads * dh), BF16),
        grid_spec=grid_spec,
        compiler_params=_params(("arbitrary", "arbitrary", "arbitrary")),
        name="fox_flash",
    )(first_pair, qk, qx, qk, kx, vt)


def _post_kernel(a_ref, w_ref, x_ref, gp_ref, gn_ref, xo_ref, ho_ref, *, parts):
    rows = a_ref.shape[0] // parts
    for p in range(parts):
        sl = slice(p * rows, (p + 1) * rows)
        mix = jnp.dot(a_ref[sl, :], w_ref[...], preferred_element_type=F32)
        x_new = x_ref[sl, :] + _rms(mix, gp_ref[...])
        xo_ref[sl, :] = x_new
        ho_ref[sl, :] = _rms(x_new, gn_ref[...]).astype(ho_ref.dtype)


def post_mixer(a, w, x, g_post, g_next, tm):
    m, k = a.shape
    d = w.shape[1]
    return pl.pallas_call(
        functools.partial(_post_kernel, parts=2),
        out_shape=(jax.ShapeDtypeStruct((m, d), F32), jax.ShapeDtypeStruct((m, d), BF16)),
        grid=(m // tm,),
        in_specs=[pl.BlockSpec((tm, k), lambda i: (i, 0)),
                  pl.BlockSpec((k, d), lambda i: (0, 0), pipeline_mode=pl.Buffered(1)),
                  pl.BlockSpec((tm, d), lambda i: (i, 0)),
                  pl.BlockSpec((1, d), lambda i: (0, 0)),
                  pl.BlockSpec((1, d), lambda i: (0, 0))],
        out_specs=(pl.BlockSpec((tm, d), lambda i: (i, 0)),
                   pl.BlockSpec((tm, d), lambda i: (i, 0))),
        compiler_params=_params(("parallel",)),
        name="post_mixer",
    )(a, w, x, g_post, g_next)


def _ffn_kernel(h_ref, halo_ref, x_ref, wa_ref, wb_ref, wd_ref, cwa_ref, cwb_ref,
                cba_ref, cbb_ref, gp_ref, gn_ref, xo_ref, *rest, tiles_per_seq, emit_next):
    if emit_next:
        ho_ref, hs_ref, acc_ref = rest
    else:
        hs_ref, acc_ref = rest
    i = pl.program_id(0)
    j = pl.program_id(1)
    nj = pl.num_programs(1)
    tm = h_ref.shape[0]
    pad = halo_ref.shape[0]

    @pl.when(j == 0)
    def _():
        halo = halo_ref[...]
        hs_ref[0:pad, :] = jnp.where(i % tiles_per_seq != 0, halo, jnp.zeros_like(halo))
        hs_ref[pad:pad + tm, :] = h_ref[...]
        acc_ref[...] = jnp.zeros_like(acc_ref)

    hs = hs_ref[...]

    def conv(w_ref, cw_ref, cb_ref):
        u = jnp.dot(hs, w_ref[...], preferred_element_type=F32)
        u1 = pltpu.roll(u, 1, 0)
        u2 = pltpu.roll(u, 2, 0)
        cw = cw_ref[...]
        return (u[pad:, :] * cw[2:3, :] + u1[pad:, :] * cw[1:2, :]
                + u2[pad:, :] * cw[0:1, :] + cb_ref[...])

    a = conv(wa_ref, cwa_ref, cba_ref)
    b = conv(wb_ref, cwb_ref, cbb_ref)
    act = (a * jax.nn.sigmoid(a) * b).astype(BF16)
    acc_ref[...] += jnp.dot(act, wd_ref[...], preferred_element_type=F32)

    @pl.when(j == nj - 1)
    def _():
        x_new = x_ref[...] + _rms(acc_ref[...], gp_ref[...])
        xo_ref[...] = x_new
        if emit_next:
            ho_ref[...] = _rms(x_new, gn_ref[...]).astype(ho_ref.dtype)


def conv_ffn(h, x, layer, w_up, conv_w, conv_b, w_down, g_post, g_next, seq, tm, tf, emit_next):
    m, d = h.shape
    f = w_down.shape[1]
    nf = f // tf
    pad = BF16_SUBLANES
    tiles_per_seq = seq // tm
    halo_blocks = tm // pad
    kern = functools.partial(_ffn_kernel, tiles_per_seq=tiles_per_seq, emit_next=emit_next)
    out_shape = [jax.ShapeDtypeStruct((m, d), F32)]
    out_specs = [pl.BlockSpec((tm, d), lambda i, j: (i, 0))]
    if emit_next:
        out_shape.append(jax.ShapeDtypeStruct((m, d), BF16))
        out_specs.append(pl.BlockSpec((tm, d), lambda i, j: (i, 0)))
    res = pl.pallas_call(
        kern,
        out_shape=tuple(out_shape),
        grid=(m // tm, nf),
        in_specs=[pl.BlockSpec((tm, d), lambda i, j: (i, 0)),
                  pl.BlockSpec((pad, d), lambda i, j: (jnp.maximum(i * halo_blocks - 1, 0), 0)),
                  pl.BlockSpec((tm, d), lambda i, j: (i, 0)),
                  pl.BlockSpec((None, d, tf), lambda i, j: (layer, 0, j)),
                  pl.BlockSpec((None, d, tf), lambda i, j: (layer, 0, nf + j)),
                  pl.BlockSpec((None, tf, d), lambda i, j: (layer, j, 0)),
                  pl.BlockSpec((None, conv_w.shape[1], tf), lambda i, j: (layer, 0, j)),
                  pl.BlockSpec((None, conv_w.shape[1], tf), lambda i, j: (layer, 0, nf + j)),
                  pl.BlockSpec((None, 1, tf), lambda i, j: (layer, 0, j)),
                  pl.BlockSpec((None, 1, tf), lambda i, j: (layer, 0, nf + j)),
                  pl.BlockSpec((1, d), lambda i, j: (0, 0)),
                  pl.BlockSpec((1, d), lambda i, j: (0, 0))],
        out_specs=tuple(out_specs),
        scratch_shapes=[pltpu.VMEM((pad + tm, d), BF16), pltpu.VMEM((tm, d), F32)],
        compiler_params=_params(("arbitrary", "arbitrary")),
        name="conv_ffn",
    )(h, h, x, w_up, w_up, w_down, conv_w, conv_w, conv_b, conv_b, g_post, g_next)
    return res if emit_next else (res[0], None)


def _ret_proj_kernel(h_ref, wqk_ref, wvg_ref, cos_ref, sin_ref, o_ref, *, n_q_tiles, n_qk_tiles,
                     n_qkv_tiles, k_scale):
    j = pl.program_id(1)
    tn = wqk_ref.shape[1]

    def product(w_ref=wvg_ref):
        return jnp.dot(h_ref[...], w_ref[...], preferred_element_type=F32)

    @pl.when(j < n_qk_tiles)
    def _():
        acc = product(wqk_ref)
        scale = jnp.where(j < n_q_tiles, 1.0, k_scale).astype(F32)
        c = cos_ref[...] * scale
        s = sin_ref[...] * scale
        for hh in range(tn // (2 * LANES)):
            lo = hh * 2 * LANES
            x1 = acc[:, lo:lo + LANES]
            x2 = acc[:, lo + LANES:lo + 2 * LANES]
            o_ref[:, lo:lo + LANES] = (x1 * c - x2 * s).astype(o_ref.dtype)
            o_ref[:, lo + LANES:lo + 2 * LANES] = (x1 * s + x2 * c).astype(o_ref.dtype)

    @pl.when((j >= n_qk_tiles) & (j < n_qkv_tiles))
    def _():
        o_ref[...] = product().astype(o_ref.dtype)

    @pl.when(j >= n_qkv_tiles)
    def _():
        acc = product()
        o_ref[...] = (acc * jax.nn.sigmoid(acc)).astype(o_ref.dtype)


def ret_proj(h, w_qk, w_vg, cos, sin, d_model, k_scale, seq, tm, tn):
    m, k = h.shape
    n = w_qk.shape[1] + w_vg.shape[1]
    tiles_per_seq = seq // tm
    half = cos.shape[1]
    n_qk_tiles = w_qk.shape[1] // tn
    kern = functools.partial(_ret_proj_kernel, n_q_tiles=d_model // tn,
                             n_qk_tiles=n_qk_tiles, n_qkv_tiles=4 * d_model // tn,
                             k_scale=k_scale)
    return pl.pallas_call(
        kern,
        out_shape=jax.ShapeDtypeStruct((m, n), BF16),
        grid=(m // tm, n // tn),
        in_specs=[pl.BlockSpec((tm, k), lambda i, j: (i, 0)),
                  pl.BlockSpec((k, tn), lambda i, j: (0, jnp.minimum(j, n_qk_tiles - 1))),
                  pl.BlockSpec((k, tn), lambda i, j: (0, jnp.maximum(j - n_qk_tiles, 0))),
                  pl.BlockSpec((tm, half), lambda i, j: (i % tiles_per_seq, 0)),
                  pl.BlockSpec((tm, half), lambda i, j: (i % tiles_per_seq, 0))],
        out_specs=pl.BlockSpec((tm, tn), lambda i, j: (i, j)),
        compiler_params=_params(("parallel", "arbitrary")),
        name="ret_proj",
    )(h, w_qk, w_vg, cos, sin)


def _retention_kernel(lg_ref, q_ref, k_ref, v_ref, g_ref, o_ref, state_ref, mask_ref, *, chunk):
    h = pl.program_id(1)
    c = pl.program_id(2)
    rows = q_ref.shape[0]
    lg = lg_ref[h]

    @pl.when(c == 0)
    def _():
        state_ref[...] = jnp.zeros_like(state_ref)
        n = lax.broadcasted_iota(jnp.int32, (chunk, chunk), 0)
        mm = lax.broadcasted_iota(jnp.int32, (chunk, chunk), 1)
        rel = (n - mm).astype(F32)
        mask_ref[...] = jnp.where(rel >= 0, jnp.exp(jnp.maximum(rel, 0.0) * lg), 0.0)

    idx = lax.broadcasted_iota(jnp.int32, (chunk, 1), 0).astype(F32)
    decay_q = jnp.exp((idx + 1.0) * lg)
    decay_k = jnp.exp((chunk - 1.0 - idx) * lg)
    decay_chunk = jnp.exp(jnp.full((1, 1), chunk, F32) * lg)

    for cc in range(rows // chunk):
        sl = slice(cc * chunk, (cc + 1) * chunk)
        qb = q_ref[sl, :]
        kb = k_ref[sl, :]
        vb = v_ref[sl, :]
        scores = lax.dot_general(qb, kb, (((1,), (1,)), ((), ())), preferred_element_type=F32)
        scores = scores * mask_ref[...]
        state = state_ref[...]
        o = (jnp.dot(scores.astype(BF16), vb, preferred_element_type=F32)
             + jnp.dot(qb, state.astype(BF16), preferred_element_type=F32) * decay_q)
        kd = (kb.astype(F32) * decay_k).astype(BF16)
        state_ref[...] = decay_chunk * state + lax.dot_general(
            kd, vb, (((0,), (0,)), ((), ())), preferred_element_type=F32)
        o = o * lax.rsqrt(jnp.mean(o * o, axis=-1, keepdims=True) + NORM_EPS)
        o_ref[sl, :] = (o * g_ref[sl, :].astype(F32)).astype(o_ref.dtype)


def retention_mix(proj, log_gamma, n_heads, d_model, batch, seq, chunk, rows):
    m = proj.shape[0]
    dk = d_model // n_heads
    dv = 2 * d_model // n_heads
    nc = seq // rows
    grid_spec = pltpu.PrefetchScalarGridSpec(
        num_scalar_prefetch=1,
        grid=(batch, n_heads, nc),
        in_specs=[pl.BlockSpec((rows, dk), lambda b, h, c, lg: (b * nc + c, h)),
                  pl.BlockSpec((rows, dk), lambda b, h, c, lg: (b * nc + c, n_heads + h)),
                  pl.BlockSpec((rows, dv), lambda b, h, c, lg: (b * nc + c, n_heads + h)),
                  pl.BlockSpec((rows, dv), lambda b, h, c, lg: (b * nc + c, 2 * n_heads + h))],
        out_specs=pl.BlockSpec((rows, dv), lambda b, h, c, lg: (b * nc + c, h)),
        scratch_shapes=[pltpu.VMEM((dk, dv), F32), pltpu.VMEM((chunk, chunk), F32)],
    )
    return pl.pallas_call(
        functools.partial(_retention_kernel, chunk=chunk),
        out_shape=jax.ShapeDtypeStruct((m, n_heads * dv), BF16),
        grid_spec=grid_spec,
        compiler_params=_params(("arbitrary", "arbitrary", "arbitrary")),
        name="retention_mix",
    )(log_gamma, proj, proj, proj, proj)


def _pick(total, pref):
    if total <= pref:
        return total
    t = pref
    while total % t:
        t -= LANES
    return t


def kernel(x, norm_g, fox_w_in, fox_b_f, fox_w_o, ret_w_in, ret_w_o,
           ffn_w_up, ffn_conv_w, ffn_conv_b, ffn_w_down):
    batch, seq, d = x.shape
    m = batch * seq
    fox_heads = fox_b_f.shape[-1]
    ret_heads = RET_HEADS
    assert d == fox_heads * LANES, "fox head dim must equal the lane width"
    dk = d // ret_heads
    assert dk == 2 * LANES, "retention qk head dim must be two lane tiles"
    f = ffn_w_down.shape[1]

    tm_norm = _pick(seq, 512)
    tm_proj = _pick(seq, 1024)
    tn_proj = _pick(d, 1024)
    tm_row = _pick(seq, 512)
    tq = _pick(seq, 1024)
    tk = _pick(seq, 512)
    tf = _pick(f, 512)
    chunk = _pick(seq, 256)

    g = norm_g.reshape(norm_g.shape[0], 4, 1, d)
    x2 = x.reshape(m, d)

    w_qk = fox_w_in[0][:, :2 * d].astype(BF16)
    w_vt = fox_w_in[0][:, 2 * d:3 * d].T.astype(BF16)
    w_f = jnp.pad(fox_w_in[0][:, 3 * d:], ((0, 0), (0, LANES - fox_heads))).astype(BF16)
    b_f = jnp.pad(fox_b_f[0], (0, LANES - fox_heads)).reshape(1, LANES)
    h = norm_cast(x2, g[0, 0], tm_norm)
    qk = fox_proj(h, w_qk, d, LANES ** -0.5 * LOG2E, tm_proj, tn_proj)
    vt = proj_transposed(h, w_vt, tk, tn_proj)
    qx, kx, f_first, f_last = forget_features(h, w_f, b_f, fox_heads, batch, seq, tk)
    sumsq = head_sumsq_max(qk, tk)
    first_pair = first_needed_pair(sumsq, f_first, f_last, fox_heads, batch, seq, tq, tk)
    attn = flash_attention(first_pair, qk, qx, kx, vt, fox_heads, batch, seq, tq, tk)
    x2, h = post_mixer(attn, fox_w_o[0].astype(BF16), x2, g[0, 1], g[0, 2], tm_row)
    w_up = ffn_w_up.astype(BF16)
    w_down = ffn_w_down.astype(BF16)
    conv_b = ffn_conv_b.reshape(ffn_conv_b.shape[0], 1, 2 * f)
    x2, h = conv_ffn(h, x2, 0, w_up, ffn_conv_w, conv_b, w_down,
                     g[0, 3], g[1, 0], seq, tm_row, tf, emit_next=True)

    w_in = ret_w_in[0]
    w_ret_qk = (w_in[:, :2 * d].reshape(d, 2 * ret_heads, dk // 2, 2).swapaxes(-1, -2)
                .reshape(d, 2 * d).astype(BF16))
    w_ret_vg = w_in[:, 2 * d:].astype(BF16)
    theta = 1.0 / (RET_ROT_BASE ** jnp.linspace(0.0, 1.0, dk // 2, dtype=F32))
    ang = jnp.arange(seq, dtype=F32)[:, None] * theta[None, :]
    log_gamma = jnp.log1p(-(2.0 ** (-5.0 - jnp.arange(ret_heads, dtype=F32))))
    proj = ret_proj(h, w_ret_qk, w_ret_vg, jnp.cos(ang), jnp.sin(ang), d, dk ** -0.5, seq,
                    tm_proj, tn_proj)
    mix = retention_mix(proj, log_gamma, ret_heads, d, batch, seq, chunk, _pick(seq, 1024))
    x2, h = post_mixer(mix, ret_w_o[0].astype(BF16), x2, g[1, 1], g[1, 2], _pick(seq, 256))
    x2, _ = conv_ffn(h, x2, 1, w_up, ffn_conv_w, conv_b, w_down,
                     g[1, 3], g[1, 3], seq, tm_row, tf, emit_next=False)
    return x2.reshape(batch, seq, d)
```

```python
import functools
import math

import jax
import jax.numpy as jnp
from jax import lax
from jax.experimental import pallas as pl
from jax.experimental.pallas import tpu as pltpu

NORM_EPS = 1e-6
RET_HEADS = 8
RET_ROT_BASE = 10000.0
LOG2E = 1.4426950408889634
NEG_BIG = -1e30

LANES = 128
BF16_SUBLANES = 16
VMEM_LIMIT = 56 * 1024 * 1024

F32 = jnp.float32
BF16 = jnp.bfloat16


def _params(sem, vmem=VMEM_LIMIT):
    return pltpu.CompilerParams(dimension_semantics=sem, vmem_limit_bytes=vmem)


def _rms(x, g):
    ms = jnp.mean(x * x, axis=-1, keepdims=True)
    return x * lax.rsqrt(ms + NORM_EPS) * g


def _norm_kernel(x_ref, g_ref, o_ref):
    o_ref[...] = _rms(x_ref[...], g_ref[...]).astype(o_ref.dtype)


def norm_cast(x, g, tm):
    m, d = x.shape
    return pl.pallas_call(
        _norm_kernel,
        out_shape=jax.ShapeDtypeStruct((m, d), BF16),
        grid=(m // tm,),
        in_specs=[pl.BlockSpec((tm, d), lambda i: (i, 0)),
                  pl.BlockSpec((1, d), lambda i: (0, 0))],
        out_specs=pl.BlockSpec((tm, d), lambda i: (i, 0)),
        compiler_params=_params(("parallel",)),
        name="norm_cast",
    )(x, g)


def _fox_proj_kernel(h_ref, w_ref, o_ref, *, n_q_tiles, q_scale):
    j = pl.program_id(1)
    acc = jnp.dot(h_ref[...], w_ref[...], preferred_element_type=F32)
    scale = jnp.where(j < n_q_tiles, q_scale, 1.0).astype(F32)
    o_ref[...] = (acc * scale).astype(o_ref.dtype)


def fox_proj(h, w, d_model, q_scale, tm, tn):
    m, k = h.shape
    n = w.shape[1]
    kern = functools.partial(_fox_proj_kernel, n_q_tiles=d_model // tn, q_scale=q_scale)
    return pl.pallas_call(
        kern,
        out_shape=jax.ShapeDtypeStruct((m, n), BF16),
        grid=(m // tm, n // tn),
        in_specs=[pl.BlockSpec((tm, k), lambda i, j: (i, 0)),
                  pl.BlockSpec((k, tn), lambda i, j: (0, j))],
        out_specs=pl.BlockSpec((tm, tn), lambda i, j: (i, j)),
        compiler_params=_params(("parallel", "arbitrary")),
        name="fox_proj",
    )(h, w)


def _proj_t_kernel(wt_ref, h_ref, o_ref):
    o_ref[0] = lax.dot_general(wt_ref[...], h_ref[...], (((1,), (1,)), ((), ())),
                               preferred_element_type=F32).astype(o_ref.dtype)


def proj_transposed(h, wt, tm, tn):
    m, k = h.shape
    n = wt.shape[0]
    return pl.pallas_call(
        _proj_t_kernel,
        out_shape=jax.ShapeDtypeStruct((m // tm, n, tm), BF16),
        grid=(m // tm, n // tn),
        in_specs=[pl.BlockSpec((tn, k), lambda i, j: (j, 0)),
                  pl.BlockSpec((tm, k), lambda i, j: (i, 0))],
        out_specs=pl.BlockSpec((1, tn, tm), lambda i, j: (i, j, 0)),
        compiler_params=_params(("parallel", "arbitrary")),
        name="fox_proj_vt",
    )(wt, h)


def _split3(x):
    p1 = x.astype(BF16)
    r1 = x - p1.astype(F32)
    p2 = r1.astype(BF16)
    p3 = (r1 - p2.astype(F32)).astype(BF16)
    return p1, p2, p3


def _forget_kernel(h_ref, wf_ref, bf_ref, sel_ref, const_ref, qx_ref, kx_ref, f0_ref, f1_ref,
                   carry_ref, *, n_heads):
    t = pl.program_id(1)
    tm = h_ref.shape[0]

    @pl.when(t == 0)
    def _():
        carry_ref[...] = jnp.zeros_like(carry_ref)

    z = jnp.dot(h_ref[...], wf_ref[...], preferred_element_type=F32) + bf_ref[...]
    log_f = jnp.minimum(z, 0.0) - jnp.log1p(jnp.exp(-jnp.abs(z)))
    lane = lax.broadcasted_iota(jnp.int32, z.shape, 1)
    log_f = jnp.where(lane < n_heads, log_f * LOG2E, 0.0)

    row = lax.broadcasted_iota(jnp.int32, (tm, tm), 0)
    col = lax.broadcasted_iota(jnp.int32, (tm, tm), 1)
    tri = jnp.where(row >= col, 1.0, 0.0).astype(BF16)
    cum = carry_ref[...]
    for piece in _split3(log_f):
        cum = cum + jnp.dot(tri, piece, preferred_element_type=F32)
    carry_ref[...] = cum[tm - 1:tm, :]
    f0_ref[0] = cum[0:1, :]
    f1_ref[0] = cum[tm - 1:tm, :]

    qx = const_ref[0:1, :].astype(F32)
    kx = const_ref[1:2, :].astype(F32)
    for idx, piece in enumerate(_split3(cum)):
        qx = qx + jnp.dot(piece, sel_ref[idx], preferred_element_type=F32)
        kx = kx + jnp.dot(piece, sel_ref[3 + idx], preferred_element_type=F32)
    qx_ref[...] = qx.astype(qx_ref.dtype)
    kx_ref[...] = kx.astype(kx_ref.dtype)


FEATS_PER_HEAD = 8


def forget_features(h, wf, bf, n_heads, batch, seq, tm):
    m, d = h.shape
    assert n_heads * FEATS_PER_HEAD <= LANES
    head = jnp.arange(LANES)[:, None]
    colid = jnp.arange(LANES)[None, :]
    sel = []
    for sign, base in ((1.0, 0), (-1.0, 3)):
        for idx in range(3):
            sel.append(jnp.where((colid == head * FEATS_PER_HEAD + base + idx) & (head < n_heads),
                                 sign, 0.0))
    sel = jnp.stack(sel).astype(BF16)
    within = colid % FEATS_PER_HEAD
    used = colid < n_heads * FEATS_PER_HEAD
    const = jnp.concatenate([
        jnp.where((within >= 3) & (within < 6) & used, 1.0, 0.0),
        jnp.where((within < 3) & used, 1.0, 0.0),
    ], axis=0).astype(BF16)
    tiles = seq // tm
    kern = functools.partial(_forget_kernel, n_heads=n_heads)
    row_spec = pl.BlockSpec((tm, LANES), lambda b, t: (b * tiles + t, 0))
    edge_spec = pl.BlockSpec((1, 1, LANES), lambda b, t: (b * tiles + t, 0, 0))
    return pl.pallas_call(
        kern,
        out_shape=(jax.ShapeDtypeStruct((m, LANES), BF16),
                   jax.ShapeDtypeStruct((m, LANES), BF16),
                   jax.ShapeDtypeStruct((m // tm, 1, LANES), F32),
                   jax.ShapeDtypeStruct((m // tm, 1, LANES), F32)),
        grid=(batch, tiles),
        in_specs=[pl.BlockSpec((tm, d), lambda b, t: (b * tiles + t, 0)),
                  pl.BlockSpec((d, LANES), lambda b, t: (0, 0)),
                  pl.BlockSpec((1, LANES), lambda b, t: (0, 0)),
                  pl.BlockSpec((6, LANES, LANES), lambda b, t: (0, 0, 0)),
                  pl.BlockSpec((2, LANES), lambda b, t: (0, 0))],
        out_specs=(row_spec, row_spec, edge_spec, edge_spec),
        scratch_shapes=[pltpu.VMEM((1, LANES), F32)],
        compiler_params=_params(("arbitrary", "arbitrary")),
        name="forget_features",
    )(h, wf, bf, sel, const)


def _norms_kernel(x_ref, sel_ref, o_ref):
    x = x_ref[...].astype(F32)
    sumsq = jnp.dot((x * x).astype(BF16), sel_ref[...], preferred_element_type=F32)
    o_ref[0] = jnp.max(sumsq, axis=0, keepdims=True)


def head_sumsq_max(x, tm):
    m, width = x.shape
    groups = width // LANES
    assert groups <= LANES
    sel = (jnp.arange(width)[:, None] // LANES == jnp.arange(LANES)[None, :]).astype(BF16)
    return pl.pallas_call(
        _norms_kernel,
        out_shape=jax.ShapeDtypeStruct((m // tm, 1, LANES), F32),
        grid=(m // tm,),
        in_specs=[pl.BlockSpec((tm, width), lambda i: (i, 0)),
                  pl.BlockSpec((width, LANES), lambda i: (0, 0))],
        out_specs=pl.BlockSpec((1, 1, LANES), lambda i: (i, 0, 0)),
        compiler_params=_params(("parallel",)),
        name="qk_norms",
    )(x, sel)


SKIP_MARGIN_LOG2 = 100.0


def first_needed_pair(sumsq, f_first, f_last, n_heads, batch, seq, tq, tk):
    nk = seq // tk
    nq = seq // tq
    per_q = tq // tk
    sumsq = sumsq.reshape(batch, nk, LANES)
    qmax = jnp.sqrt(sumsq[:, :, :n_heads].reshape(batch, nq, per_q, n_heads).max(axis=2))
    kmax = jnp.sqrt(sumsq[:, :, n_heads:2 * n_heads].max(axis=1))
    need = SKIP_MARGIN_LOG2 + 2.0 * 1.05 * qmax * kmax[:, None, :] + 1.0
    f_q = f_first.reshape(batch, nq, per_q, LANES)[:, :, 0, :n_heads]
    f_k = f_last.reshape(batch, nk, LANES)[:, :, :n_heads]
    bound = f_q[:, :, None, :] - f_k[:, None, :, :]
    below_band = (jnp.arange(nk)[None, :] < per_q * jnp.arange(nq)[:, None])[None, :, :, None]
    droppable = (bound < -need[:, :, None, :]) & below_band
    first_block = jnp.argmin(droppable, axis=2)
    return (first_block // 2).transpose(0, 2, 1).reshape(-1).astype(jnp.int32)


def _flash_kernel(start_ref, q_ref, qx_ref, k_ref, kx_ref, vt_ref, o_ref, m_ref, l_ref, acc_ref,
                  s0_ref, s1_ref, *, tq, tk):
    b = pl.program_id(0)
    h = pl.program_id(1)
    i = pl.program_id(2)
    first_pair = start_ref[(b * pl.num_programs(1) + h) * pl.num_programs(2) + i]
    lane = lax.broadcasted_iota(jnp.int32, qx_ref.shape, 1)
    own = (lane >= h * FEATS_PER_HEAD) & (lane < (h + 1) * FEATS_PER_HEAD)
    qx = jnp.where(own, qx_ref[...], jnp.zeros_like(qx_ref))
    q2 = jnp.concatenate([q_ref[...], qx], axis=1)
    m_ref[...] = jnp.full_like(m_ref, NEG_BIG)
    l_ref[...] = jnp.zeros_like(l_ref)
    acc_ref[...] = jnp.zeros_like(acc_ref)

    def scores(j, s_ref, lo=0, hi=tq):
        off = pl.multiple_of(j * tk, tk)
        k2 = jnp.concatenate([k_ref[pl.ds(off, tk), :], kx_ref[pl.ds(off, tk), :]], axis=1)
        s_ref[:, lo:hi] = lax.dot_general(k2, q2[lo:hi], (((1,), (1,)), ((), ())),
                                          preferred_element_type=F32)

    def softmax_pv(j, s_ref, masked, lo=0, hi=tq):
        st = s_ref[:, lo:hi]
        if masked:
            key = j * tk + lax.broadcasted_iota(jnp.int32, st.shape, 0)
            qry = i * tq + lo + lax.broadcasted_iota(jnp.int32, st.shape, 1)
            st = jnp.where(key <= qry, st, NEG_BIG)
        m_old = m_ref[:, lo:hi]
        m_new = jnp.maximum(m_old, jnp.max(st, axis=0, keepdims=True))
        pt = jnp.exp2(st - m_new)
        alpha = jnp.exp2(m_old - m_new)
        l_ref[:, lo:hi] = alpha * l_ref[:, lo:hi] + jnp.sum(pt, axis=0, keepdims=True)
        acc_ref[:, lo:hi] = alpha * acc_ref[:, lo:hi] + jnp.dot(
            vt_ref[j], pt.astype(BF16), preferred_element_type=F32)
        m_ref[:, lo:hi] = m_new

    def pair(jj, carry):
        j = 2 * jj
        scores(j + 1, s1_ref)
        softmax_pv(j, s0_ref, masked=False)
        scores(j + 2, s0_ref)
        softmax_pv(j + 1, s1_ref, masked=False)
        return carry

    def two_pairs(qq, carry):
        pair(first_pair + 2 * qq, carry)
        pair(first_pair + 2 * qq + 1, carry)
        return carry

    n_full = 2 * i
    n_pairs = i - first_pair
    scores(2 * first_pair, s0_ref)
    lax.fori_loop(0, lax.shift_right_logical(n_pairs, 1), two_pairs, 0)

    @pl.when((n_pairs & 1) == 1)
    def _():
        pair(i - 1, 0)
    scores(n_full + 1, s1_ref, tk, tq)
    softmax_pv(n_full, s0_ref, True, 0, tk)
    softmax_pv(n_full, s0_ref, False, tk, tq)
    softmax_pv(n_full + 1, s1_ref, True, tk, tq)
    o_ref[...] = (acc_ref[...] / l_ref[...]).T.astype(o_ref.dtype)


def flash_attention(first_pair, qk, qx, kx, vt, n_heads, batch, seq, tq, tk):
    m = qk.shape[0]
    dh = LANES
    nq = seq // tq
    nk = seq // tk
    assert tq == 2 * tk, "the kernel walks key blocks in pairs, one pair per query block"
    kern = functools.partial(_flash_kernel, tq=tq, tk=tk)
    grid_spec = pltpu.PrefetchScalarGridSpec(
        num_scalar_prefetch=1,
        grid=(batch, n_heads, nq),
        in_specs=[pl.BlockSpec((tq, dh), lambda b, h, i, s: (b * nq + i, h)),
                  pl.BlockSpec((tq, dh), lambda b, h, i, s: (b * nq + i, 0)),
                  pl.BlockSpec((seq, dh), lambda b, h, i, s: (b, n_heads + h)),
                  pl.BlockSpec((seq, dh), lambda b, h, i, s: (b, 0)),
                  pl.BlockSpec((nk, dh, tk), lambda b, h, i, s: (b, h, 0))],
        out_specs=pl.BlockSpec((tq, dh), lambda b, h, i, s: (b * nq + i, h)),
        scratch_shapes=[pltpu.VMEM((1, tq), F32), pltpu.VMEM((1, tq), F32),
                        pltpu.VMEM((dh, tq), F32),
                        pltpu.VMEM((tk, tq), F32), pltpu.VMEM((tk, tq), F32)],
    )
    return pl.pallas_call(
        kern,
        out_shape=jax.ShapeDtypeStruct((m, n_heads * dh), BF16),
        grid_spec=grid_spec,
        compiler_params=_params(("arbitrary", "arbitrary", "arbitrary")),
        name="fox_flash",
    )(first_pair, qk, qx, qk, kx, vt)


def _post_kernel(a_ref, w_ref, x_ref, gp_ref, gn_ref, xo_ref, ho_ref, *, parts):
    rows = a_ref.shape[0] // parts
    for p in range(parts):
        sl = slice(p * rows, (p + 1) * rows)
        mix = jnp.dot(a_ref[sl, :], w_ref[...], preferred_element_type=F32)
        x_new = x_ref[sl, :] + _rms(mix, gp_ref[...])
        xo_ref[sl, :] = x_new
        ho_ref[sl, :] = _rms(x_new, gn_ref[...]).astype(ho_ref.dtype)


def post_mixer(a, w, x, g_post, g_next, tm):
    m, k = a.shape
    d = w.shape[1]
    return pl.pallas_call(
        functools.partial(_post_kernel, parts=2),
        out_shape=(jax.ShapeDtypeStruct((m, d), F32), jax.ShapeDtypeStruct((m, d), BF16)),
        grid=(m // tm,),
        in_specs=[pl.BlockSpec((tm, k), lambda i: (i, 0)),
                  pl.BlockSpec((k, d), lambda i: (0, 0), pipeline_mode=pl.Buffered(1)),
                  pl.BlockSpec((tm, d), lambda i: (i, 0)),
                  pl.BlockSpec((1, d), lambda i: (0, 0)),
                  pl.BlockSpec((1, d), lambda i: (0, 0))],
        out_specs=(pl.BlockSpec((tm, d), lambda i: (i, 0)),
                   pl.BlockSpec((tm, d), lambda i: (i, 0))),
        compiler_params=_params(("parallel",)),
        name="post_mixer",
    )(a, w, x, g_post, g_next)


def _ffn_kernel(h_ref, halo_ref, x_ref, wa_ref, wb_ref, wd_ref, cwa_ref, cwb_ref,
                cba_ref, cbb_ref, gp_ref, gn_ref, xo_ref, *rest, tiles_per_seq, emit_next):
    if emit_next:
        ho_ref, hs_ref, acc_ref = rest
    else:
        hs_ref, acc_ref = rest
    i = pl.program_id(0)
    j = pl.program_id(1)
    nj = pl.num_programs(1)
    tm = h_ref.shape[0]
    pad = halo_ref.shape[0]

    @pl.when(j == 0)
    def _():
        halo = halo_ref[...]
        hs_ref[0:pad, :] = jnp.where(i % tiles_per_seq != 0, halo, jnp.zeros_like(halo))
        hs_ref[pad:pad + tm, :] = h_ref[...]
        acc_ref[...] = jnp.zeros_like(acc_ref)

    hs = hs_ref[...]

    def conv(w_ref, cw_ref, cb_ref):
        u = jnp.dot(hs, w_ref[...], preferred_element_type=F32)
        u1 = pltpu.roll(u, 1, 0)
        u2 = pltpu.roll(u, 2, 0)
        cw = cw_ref[...]
        return (u[pad:, :] * cw[2:3, :] + u1[pad:, :] * cw[1:2, :]
                + u2[pad:, :] * cw[0:1, :] + cb_ref[...])

    a = conv(wa_ref, cwa_ref, cba_ref)
    b = conv(wb_ref, cwb_ref, cbb_ref)
    act = (a * jax.nn.sigmoid(a) * b).astype(BF16)
    acc_ref[...] += jnp.dot(act, wd_ref[...], preferred_element_type=F32)

    @pl.when(j == nj - 1)
    def _():
        x_new = x_ref[...] + _rms(acc_ref[...], gp_ref[...])
        xo_ref[...] = x_new
        if emit_next:
            ho_ref[...] = _rms(x_new, gn_ref[...]).astype(ho_ref.dtype)


def conv_ffn(h, x, layer, w_up, conv_w, conv_b, w_down, g_post, g_next, seq, tm, tf, emit_next):
    m, d = h.shape
    f = w_down.shape[1]
    nf = f // tf
    pad = BF16_SUBLANES
    tiles_per_seq = seq // tm
    halo_blocks = tm // pad
    kern = functools.partial(_ffn_kernel, tiles_per_seq=tiles_per_seq, emit_next=emit_next)
    out_shape = [jax.ShapeDtypeStruct((m, d), F32)]
    out_specs = [pl.BlockSpec((tm, d), lambda i, j: (i, 0))]
    if emit_next:
        out_shape.append(jax.ShapeDtypeStruct((m, d), BF16))
        out_specs.append(pl.BlockSpec((tm, d), lambda i, j: (i, 0)))
    res = pl.pallas_call(
        kern,
        out_shape=tuple(out_shape),
        grid=(m // tm, nf),
        in_specs=[pl.BlockSpec((tm, d), lambda i, j: (i, 0)),
                  pl.BlockSpec((pad, d), lambda i, j: (jnp.maximum(i * halo_blocks - 1, 0), 0)),
                  pl.BlockSpec((tm, d), lambda i, j: (i, 0)),
                  pl.BlockSpec((None, d, tf), lambda i, j: (layer, 0, j)),
                  pl.BlockSpec((None, d, tf), lambda i, j: (layer, 0, nf + j)),
                  pl.BlockSpec((None, tf, d), lambda i, j: (layer, j, 0)),
                  pl.BlockSpec((None, conv_w.shape[1], tf), lambda i, j: (layer, 0, j)),
                  pl.BlockSpec((None, conv_w.shape[1], tf), lambda i, j: (layer, 0, nf + j)),
                  pl.BlockSpec((None, 1, tf), lambda i, j: (layer, 0, j)),
                  pl.BlockSpec((None, 1, tf), lambda i, j: (layer, 0, nf + j)),
                  pl.BlockSpec((1, d), lambda i, j: (0, 0)),
                  pl.BlockSpec((1, d), lambda i, j: (0, 0))],
        out_specs=tuple(out_specs),
        scratch_shapes=[pltpu.VMEM((pad + tm, d), BF16), pltpu.VMEM((tm, d), F32)],
        compiler_params=_params(("arbitrary", "arbitrary")),
        name="conv_ffn",
    )(h, h, x, w_up, w_up, w_down, conv_w, conv_w, conv_b, conv_b, g_post, g_next)
    return res if emit_next else (res[0], None)


def _ret_proj_kernel(h_ref, w_ref, cos_ref, sin_ref, o_ref, *, n_q_tiles, n_qk_tiles,
                     n_qkv_tiles, k_scale):
    j = pl.program_id(1)
    tn = w_ref.shape[1]

    def product():
        return jnp.dot(h_ref[...], w_ref[...], preferred_element_type=F32)

    @pl.when(j < n_qk_tiles)
    def _():
        acc = product()
        scale = jnp.where(j < n_q_tiles, 1.0, k_scale).astype(F32)
        c = cos_ref[...] * scale
        s = sin_ref[...] * scale
        for hh in range(tn // (2 * LANES)):
            lo = hh * 2 * LANES
            x1 = acc[:, lo:lo + LANES]
            x2 = acc[:, lo + LANES:lo + 2 * LANES]
            o_ref[:, lo:lo + LANES] = (x1 * c - x2 * s).astype(o_ref.dtype)
            o_ref[:, lo + LANES:lo + 2 * LANES] = (x1 * s + x2 * c).astype(o_ref.dtype)

    @pl.when((j >= n_qk_tiles) & (j < n_qkv_tiles))
    def _():
        o_ref[...] = product().astype(o_ref.dtype)

    @pl.when(j >= n_qkv_tiles)
    def _():
        acc = product()
        o_ref[...] = (acc * jax.nn.sigmoid(acc)).astype(o_ref.dtype)


def ret_proj(h, w, cos, sin, d_model, k_scale, seq, tm, tn):
    m, k = h.shape
    n = w.shape[1]
    tiles_per_seq = seq // tm
    half = cos.shape[1]
    kern = functools.partial(_ret_proj_kernel, n_q_tiles=d_model // tn,
                             n_qk_tiles=2 * d_model // tn, n_qkv_tiles=4 * d_model // tn,
                             k_scale=k_scale)
    return pl.pallas_call(
        kern,
        out_shape=jax.ShapeDtypeStruct((m, n), BF16),
        grid=(m // tm, n // tn),
        in_specs=[pl.BlockSpec((tm, k), lambda i, j: (i, 0)),
                  pl.BlockSpec((k, tn), lambda i, j: (0, j)),
                  pl.BlockSpec((tm, half), lambda i, j: (i % tiles_per_seq, 0)),
                  pl.BlockSpec((tm, half), lambda i, j: (i % tiles_per_seq, 0))],
        out_specs=pl.BlockSpec((tm, tn), lambda i, j: (i, j)),
        compiler_params=_params(("parallel", "arbitrary")),
        name="ret_proj",
    )(h, w, cos, sin)


def _retention_kernel(lg_ref, q_ref, k_ref, v_ref, g_ref, o_ref, state_ref, mask_ref, *, chunk):
    h = pl.program_id(1)
    c = pl.program_id(2)
    rows = q_ref.shape[0]
    lg = lg_ref[h]

    @pl.when(c == 0)
    def _():
        state_ref[...] = jnp.zeros_like(state_ref)
        n = lax.broadcasted_iota(jnp.int32, (chunk, chunk), 0)
        mm = lax.broadcasted_iota(jnp.int32, (chunk, chunk), 1)
        rel = (n - mm).astype(F32)
        mask_ref[...] = jnp.where(rel >= 0, jnp.exp(jnp.maximum(rel, 0.0) * lg), 0.0)

    idx = lax.broadcasted_iota(jnp.int32, (chunk, 1), 0).astype(F32)
    decay_q = jnp.exp((idx + 1.0) * lg)
    decay_k = jnp.exp((chunk - 1.0 - idx) * lg)
    decay_chunk = jnp.exp(jnp.full((1, 1), chunk, F32) * lg)

    for cc in range(rows // chunk):
        sl = slice(cc * chunk, (cc + 1) * chunk)
        qb = q_ref[sl, :]
        kb = k_ref[sl, :]
        vb = v_ref[sl, :]
        scores = lax.dot_general(qb, kb, (((1,), (1,)), ((), ())), preferred_element_type=F32)
        scores = scores * mask_ref[...]
        state = state_ref[...]
        o = (jnp.dot(scores.astype(BF16), vb, preferred_element_type=F32)
             + jnp.dot(qb, state.astype(BF16), preferred_element_type=F32) * decay_q)
        kd = (kb.astype(F32) * decay_k).astype(BF16)
        state_ref[...] = decay_chunk * state + lax.dot_general(
            kd, vb, (((0,), (0,)), ((), ())), preferred_element_type=F32)
        o = o * lax.rsqrt(jnp.mean(o * o, axis=-1, keepdims=True) + NORM_EPS)
        o_ref[sl, :] = (o * g_ref[sl, :].astype(F32)).astype(o_ref.dtype)


def retention_mix(proj, log_gamma, n_heads, d_model, batch, seq, chunk, rows):
    m = proj.shape[0]
    dk = d_model // n_heads
    dv = 2 * d_model // n_heads
    nc = seq // rows
    grid_spec = pltpu.PrefetchScalarGridSpec(
        num_scalar_prefetch=1,
        grid=(batch, n_heads, nc),
        in_specs=[pl.BlockSpec((rows, dk), lambda b, h, c, lg: (b * nc + c, h)),
                  pl.BlockSpec((rows, dk), lambda b, h, c, lg: (b * nc + c, n_heads + h)),
                  pl.BlockSpec((rows, dv), lambda b, h, c, lg: (b * nc + c, n_heads + h)),
                  pl.BlockSpec((rows, dv), lambda b, h, c, lg: (b * nc + c, 2 * n_heads + h))],
        out_specs=pl.BlockSpec((rows, dv), lambda b, h, c, lg: (b * nc + c, h)),
        scratch_shapes=[pltpu.VMEM((dk, dv), F32), pltpu.VMEM((chunk, chunk), F32)],
    )
    return pl.pallas_call(
        functools.partial(_retention_kernel, chunk=chunk),
        out_shape=jax.ShapeDtypeStruct((m, n_heads * dv), BF16),
        grid_spec=grid_spec,
        compiler_params=_params(("arbitrary", "arbitrary", "arbitrary")),
        name="retention_mix",
    )(log_gamma, proj, proj, proj, proj)


def _pick(total, pref):
    if total <= pref:
        return total
    t = pref
    while total % t:
        t -= LANES
    return t


def kernel(x, norm_g, fox_w_in, fox_b_f, fox_w_o, ret_w_in, ret_w_o,
           ffn_w_up, ffn_conv_w, ffn_conv_b, ffn_w_down):
    batch, seq, d = x.shape
    m = batch * seq
    fox_heads = fox_b_f.shape[-1]
    ret_heads = RET_HEADS
    assert d == fox_heads * LANES, "fox head dim must equal the lane width"
    dk = d // ret_heads
    assert dk == 2 * LANES, "retention qk head dim must be two lane tiles"
    f = ffn_w_down.shape[1]

    tm_norm = _pick(seq, 512)
    tm_proj = _pick(seq, 1024)
    tn_proj = _pick(d, 1024)
    tm_row = _pick(seq, 512)
    tq = _pick(seq, 1024)
    tk = _pick(seq, 512)
    tf = _pick(f, 512)
    chunk = _pick(seq, 256)

    g = norm_g.reshape(norm_g.shape[0], 4, 1, d)
    x2 = x.reshape(m, d)

    w_qk = fox_w_in[0][:, :2 * d].astype(BF16)
    w_vt = fox_w_in[0][:, 2 * d:3 * d].T.astype(BF16)
    w_f = jnp.pad(fox_w_in[0][:, 3 * d:], ((0, 0), (0, LANES - fox_heads))).astype(BF16)
    b_f = jnp.pad(fox_b_f[0], (0, LANES - fox_heads)).reshape(1, LANES)
    h = norm_cast(x2, g[0, 0], tm_norm)
    qk = fox_proj(h, w_qk, d, LANES ** -0.5 * LOG2E, tm_proj, tn_proj)
    vt = proj_transposed(h, w_vt, tk, tn_proj)
    qx, kx, f_first, f_last = forget_features(h, w_f, b_f, fox_heads, batch, seq, tk)
    sumsq = head_sumsq_max(qk, tk)
    first_pair = first_needed_pair(sumsq, f_first, f_last, fox_heads, batch, seq, tq, tk)
    attn = flash_attention(first_pair, qk, qx, kx, vt, fox_heads, batch, seq, tq, tk)
    x2, h = post_mixer(attn, fox_w_o[0].astype(BF16), x2, g[0, 1], g[0, 2], tm_row)
    w_up = ffn_w_up.astype(BF16)
    w_down = ffn_w_down.astype(BF16)
    conv_b = ffn_conv_b.reshape(ffn_conv_b.shape[0], 1, 2 * f)
    x2, h = conv_ffn(h, x2, 0, w_up, ffn_conv_w, conv_b, w_down,
                     g[0, 3], g[1, 0], seq, tm_row, tf, emit_next=True)

    perm = jnp.concatenate([jnp.arange(0, dk, 2), jnp.arange(1, dk, 2)])
    perm = (jnp.arange(ret_heads)[:, None] * dk + perm[None, :]).reshape(-1)
    w_in = ret_w_in[0]
    w_ret = jnp.concatenate([w_in[:, :d][:, perm], w_in[:, d:2 * d][:, perm], w_in[:, 2 * d:]],
                            axis=1).astype(BF16)
    theta = 1.0 / (RET_ROT_BASE ** jnp.linspace(0.0, 1.0, dk // 2, dtype=F32))
    ang = jnp.arange(seq, dtype=F32)[:, None] * theta[None, :]
    log_gamma = jnp.log1p(-(2.0 ** (-5.0 - jnp.arange(ret_heads, dtype=F32))))
    proj = ret_proj(h, w_ret, jnp.cos(ang), jnp.sin(ang), d, dk ** -0.5, seq, tm_proj, tn_proj)
    mix = retention_mix(proj, log_gamma, ret_heads, d, batch, seq, chunk, _pick(seq, 2048))
    x2, h = post_mixer(mix, ret_w_o[0].astype(BF16), x2, g[1, 1], g[1, 2], tm_row)
    x2, _ = conv_ffn(h, x2, 1, w_up, ffn_conv_w, conv_b, w_down,
                     g[1, 3], g[1, 3], seq, tm_row, tf, emit_next=False)
    return x2.reshape(batch, seq, d)
```
